```python
import jax, jax.numpy as jnp
from jax import lax
import numpy as np

D_MODEL = 1024
BATCH = 4
SEQ = 8192
DEPTH = 4

N_MIXERS = 2
N_LAYERS_A = (DEPTH + N_MIXERS - 1) // N_MIXERS
N_LAYERS_B = DEPTH // N_MIXERS
LRU_WIDTH = D_MODEL
LRU_HEADS = 4
LRU_HEAD_DIM = LRU_WIDTH // LRU_HEADS
CONV_WIDTH = 4
LRU_C = 8.0
MIN_RAD = 0.9
MAX_RAD = 0.999
POOL_WINDOWS = (2, 4, 8, 16)
POOL_GROUPS = len(POOL_WINDOWS)
POOL_GROUP_DIM = D_MODEL // POOL_GROUPS
D_FF = 4 * D_MODEL
N_MOD = 6
EPS = 1e-6

kernel_name = "hybrid_rglru_multiscale_pool_adaln"


def rms_norm(x, g):
    xf = x.astype(jnp.float32)
    y = xf * lax.rsqrt(jnp.mean(xf * xf, axis=-1, keepdims=True) + EPS)
    return (y * g.astype(jnp.float32)).astype(x.dtype)


def modulate(h, shift, scale):
    return h * (1.0 + scale[:, None, :]) + shift[:, None, :]


def causal_depthwise_conv(x, w, b):
    s = x.shape[1]
    xp = jnp.pad(x, ((0, 0), (CONV_WIDTH - 1, 0), (0, 0)))
    y = xp[:, 0:s] * w[0]
    for k in range(1, CONV_WIDTH):
        y = y + xp[:, k:k + s] * w[k]
    return y + b


def _lru_combine(left, right):
    a1, b1 = left
    a2, b2 = right
    return a1 * a2, a2 * b1 + b2


def block_diag_linear(x, w, b):
    bsz, s, _ = x.shape
    xh = x.reshape(bsz, s, LRU_HEADS, LRU_HEAD_DIM)
    y = jnp.einsum("bshi,hij->bshj", xh, w) + b
    return y.reshape(bsz, s, LRU_WIDTH)


def rg_lru(x, w_a, b_a, w_x, b_x, lam):
    gate_r = jax.nn.sigmoid(block_diag_linear(x, w_a, b_a)).astype(jnp.float32)
    gate_i = jax.nn.sigmoid(block_diag_linear(x, w_x, b_x)).astype(jnp.float32)
    log_a = LRU_C * gate_r * jax.nn.log_sigmoid(lam.astype(jnp.float32))
    a = jnp.exp(log_a)
    mult = jnp.sqrt(-jnp.expm1(2.0 * log_a))
    u = mult * (gate_i * x.astype(jnp.float32))
    _, h = lax.associative_scan(_lru_combine, (a, u), axis=1)
    return h.astype(x.dtype)


def recurrent_mixer(h, w_y, b_y, w_in, b_in, conv_w, conv_b, w_a, b_a, w_x, b_x, lam, w_out, b_out):
    gate_branch = jax.nn.gelu(jnp.einsum("bsd,dw->bsw", h, w_y) + b_y)
    xr = jnp.einsum("bsd,dw->bsw", h, w_in) + b_in
    xr = causal_depthwise_conv(xr, conv_w, conv_b)
    xr = rg_lru(xr, w_a, b_a, w_x, b_x, lam)
    return jnp.einsum("bsw,wd->bsd", xr * gate_branch, w_out) + b_out


def pool_mixer(h, w_pool, pool_scale):
    bsz, s, _ = h.shape
    hf = h.astype(jnp.float32)
    counts = jnp.arange(1, s + 1, dtype=jnp.float32)
    outs = []
    for g, win in enumerate(POOL_WINDOWS):
        xg = hf[..., g * POOL_GROUP_DIM:(g + 1) * POOL_GROUP_DIM]
        cs = jnp.cumsum(xg, axis=1)
        cs_lag = jnp.pad(cs, ((0, 0), (win, 0), (0, 0)))[:, :s]
        mean = (cs - cs_lag) / jnp.minimum(counts, float(win))[None, :, None]
        outs.append(mean - xg)
    pooled = jnp.stack(outs, axis=2).astype(h.dtype)
    mixed = jnp.einsum("bsgi,gij->bsgj", pooled, w_pool).reshape(bsz, s, D_MODEL)
    return mixed * pool_scale


def sq_relu_mlp(h, w1, w2):
    u = jax.nn.relu(jnp.einsum("bsd,df->bsf", h, w1))
    return jnp.einsum("bsf,fd->bsd", u * u, w2)


def setup_inputs(seed: int = 0) -> dict:
    key = jax.random.key(seed)
    ks = jax.random.split(key, 26)
    f32 = jnp.float32
    nrm = lambda k, shape, s: (jax.random.normal(k, shape, f32) * s)
    d, w, hd, na, nb = D_MODEL, LRU_WIDTH, LRU_HEAD_DIM, N_LAYERS_A, N_LAYERS_B
    rad = jnp.sqrt(jax.random.uniform(ks[15], (na, w), f32, MIN_RAD ** 2, MAX_RAD ** 2))
    return {
        "x": nrm(ks[0], (BATCH, SEQ, d), 1.0),
        "c": nrm(ks[1], (BATCH, d), 1.0),
        "w_mod": nrm(ks[2], (DEPTH, d, N_MOD * d), 0.5 * d ** -0.5),
        "b_mod": nrm(ks[3], (DEPTH, N_MOD * d), 0.02),
        "norm_mix_g": 1.0 + nrm(ks[4], (DEPTH, d), 0.05),
        "norm_ffn_g": 1.0 + nrm(ks[5], (DEPTH, d), 0.05),
        "lru_w_y": nrm(ks[6], (na, d, w), d ** -0.5),
        "lru_b_y": nrm(ks[7], (na, w), 0.02),
        "lru_w_in": nrm(ks[8], (na, d, w), d ** -0.5),
        "lru_b_in": nrm(ks[9], (na, w), 0.02),
        "lru_conv_w": nrm(ks[10], (na, CONV_WIDTH, w), CONV_WIDTH ** -0.5),
        "lru_conv_b": nrm(ks[11], (na, w), 0.02),
        "lru_w_a": nrm(ks[12], (na, LRU_HEADS, hd, hd), hd ** -0.5),
        "lru_b_a": nrm(ks[13], (na, LRU_HEADS, hd), 0.02),
        "lru_w_x": nrm(ks[14], (na, LRU_HEADS, hd, hd), hd ** -0.5),
        "lru_b_x": nrm(ks[16], (na, LRU_HEADS, hd), 0.02),
        "lru_lambda": jnp.log(rad) - jnp.log1p(-rad),
        "lru_w_out": nrm(ks[17], (na, w, d), w ** -0.5),
        "lru_b_out": nrm(ks[18], (na, d), 0.02),
        "pool_w": nrm(ks[19], (nb, POOL_GROUPS, POOL_GROUP_DIM, POOL_GROUP_DIM), POOL_GROUP_DIM ** -0.5),
        "pool_scale": 1.0 + nrm(ks[20], (nb, d), 0.1),
        "ffn_w1": nrm(ks[21], (DEPTH, d, D_FF), d ** -0.5),
        "ffn_w2": nrm(ks[22], (DEPTH, D_FF, d), D_FF ** -0.5),
        "final_norm_g": 1.0 + nrm(ks[23], (d,), 0.05),
    }


def reference(x, c, w_mod, b_mod, norm_mix_g, norm_ffn_g, lru_w_y, lru_b_y, lru_w_in, lru_b_in,
              lru_conv_w, lru_conv_b, lru_w_a, lru_b_a, lru_w_x, lru_b_x, lru_lambda, lru_w_out,
              lru_b_out, pool_w, pool_scale, ffn_w1, ffn_w2, final_norm_g):
    cond = jax.nn.silu(c)
    for i in range(DEPTH):
        mod = jnp.einsum("bd,de->be", cond, w_mod[i]) + b_mod[i]
        sh_m, sc_m, gt_m, sh_f, sc_f, gt_f = jnp.split(mod, N_MOD, axis=-1)
        h = modulate(rms_norm(x, norm_mix_g[i]), sh_m, sc_m)
        j = i // N_MIXERS
        if i % N_MIXERS == 0:
            y = recurrent_mixer(h, lru_w_y[j], lru_b_y[j], lru_w_in[j], lru_b_in[j],
                                lru_conv_w[j], lru_conv_b[j], lru_w_a[j], lru_b_a[j],
                                lru_w_x[j], lru_b_x[j], lru_lambda[j], lru_w_out[j], lru_b_out[j])
        else:
            y = pool_mixer(h, pool_w[j], pool_scale[j])
        x = x + gt_m[:, None, :] * y
        h = modulate(rms_norm(x, norm_ffn_g[i]), sh_f, sc_f)
        x = x + gt_f[:, None, :] * sq_relu_mlp(h, ffn_w1[i], ffn_w2[i])
    return rms_norm(x, final_norm_g)
```

```python
import functools

import jax
import jax.numpy as jnp
from jax import lax
from jax.experimental import pallas as pl
from jax.experimental.pallas import tpu as pltpu

F32 = jnp.float32
BF16 = jnp.bfloat16

N_MIXERS = 2
N_MOD = 6
CONV_WIDTH = 4
LRU_C = 8.0
POOL_WINDOWS = (2, 4, 8, 16)
EPS = 1e-6

SUBLANES = 8
SEQ_TILE = 512
FF_CHUNK = 1024
MOD_COLS = 1536
POOL_HALO = max(POOL_WINDOWS)
VMEM_LIMIT = 52 * 1024 * 1024


def _const_spec(shape):
    nd = len(shape)
    return pl.BlockSpec(shape, lambda b, s: (0,) * nd, pipeline_mode=pl.Buffered(1))


def _tile_spec(ts, d):
    return pl.BlockSpec((None, ts, d), lambda b, s: (b, s, 0))


def _mod_spec(d):
    return pl.BlockSpec((None, N_MOD, d), lambda b, s: (b, 0, 0))


def _params():
    return pltpu.CompilerParams(
        dimension_semantics=("arbitrary", "arbitrary"), vmem_limit_bytes=VMEM_LIMIT)


def _rms_scale(x):
    return lax.rsqrt(jnp.mean(x * x, axis=-1, keepdims=True) + EPS)


def _norm_modulate(x, g, shift, scale):
    return (x * _rms_scale(x)) * (g * (1.0 + scale)) + shift


def _mod_kernel(c_ref, w_ref, b_ref, o_ref):
    c = c_ref[...]
    cond = c * jax.nn.sigmoid(c)
    o_ref[...] = jnp.dot(cond.astype(BF16), w_ref[...].astype(BF16),
                         preferred_element_type=F32) + b_ref[...]


def _modulation(c, w_mod, b_mod):
    depth, d, e = w_mod.shape
    bsz = c.shape[0]
    rows = -(-bsz // SUBLANES) * SUBLANES
    c_pad = jnp.zeros((rows, d), F32).at[:bsz].set(c)
    out = pl.pallas_call(
        _mod_kernel,
        grid=(depth, e // MOD_COLS),
        in_specs=[
            pl.BlockSpec((rows, d), lambda i, j: (0, 0)),
            pl.BlockSpec((None, d, MOD_COLS), lambda i, j: (i, 0, j)),
            pl.BlockSpec((None, 1, MOD_COLS), lambda i, j: (i, 0, j)),
        ],
        out_specs=pl.BlockSpec((None, rows, MOD_COLS), lambda i, j: (i, 0, j)),
        out_shape=jax.ShapeDtypeStruct((depth, rows, e), F32),
        compiler_params=pltpu.CompilerParams(
            dimension_semantics=("arbitrary", "arbitrary"), vmem_limit_bytes=VMEM_LIMIT),
        name="adaln_mod",
    )(c_pad, w_mod, b_mod.reshape(depth, 1, e))
    return out[:, :bsz].reshape(depth, bsz, N_MOD, d)


def _mlp_kernel(*refs, final):
    if final:
        x_ref, mod_ref, g_ref, w1_ref, w2_ref, gf_ref, o_ref = refs
    else:
        x_ref, mod_ref, g_ref, w1_ref, w2_ref, o_ref = refs
    x = x_ref[...]
    h = _norm_modulate(x, g_ref[...], mod_ref[3:4, :], mod_ref[4:5, :]).astype(BF16)
    d_ff = w1_ref.shape[1]
    acc = None
    for k in range(d_ff // FF_CHUNK):
        cols = slice(k * FF_CHUNK, (k + 1) * FF_CHUNK)
        u = jnp.maximum(jnp.dot(h, w1_ref[:, cols], preferred_element_type=F32), 0.0)
        p = jnp.dot((u * u).astype(BF16), w2_ref[cols, :], preferred_element_type=F32)
        acc = p if acc is None else acc + p
    y = x + mod_ref[5:6, :] * acc
    if final:
        y = (y * _rms_scale(y)) * gf_ref[...]
    o_ref[...] = y


def _mlp_layer(x, mod, g, w1, w2, final_g=None):
    bsz, seq, d = x.shape
    final = final_g is not None
    operands = [x, mod, g.reshape(1, d), w1, w2]
    in_specs = [_tile_spec(SEQ_TILE, d), _mod_spec(d), _const_spec((1, d)),
                _const_spec(w1.shape), _const_spec(w2.shape)]
    if final:
        operands.append(final_g.reshape(1, d))
        in_specs.append(_const_spec((1, d)))
    return pl.pallas_call(
        functools.partial(_mlp_kernel, final=final),
        grid=(bsz, seq // SEQ_TILE),
        in_specs=in_specs,
        out_specs=_tile_spec(SEQ_TILE, d),
        out_shape=jax.ShapeDtypeStruct(x.shape, F32),
        compiler_params=_params(),
        name="mlp_final" if final else "mlp",
    )(*operands)


def _lru_kernel(x_ref, mod_ref, g_ref, wy_ref, by_ref, win_ref, bin_ref, cw_ref, cb_ref,
                wax_ref, bax_ref, lam_ref, wout_ref, bout_ref, o_ref,
                xbuf, a_buf, u_buf, h_carry):
    ts, width = a_buf.shape
    heads, hd, _ = wax_ref.shape

    @pl.when(pl.program_id(1) == 0)
    def _():
        xbuf[0:SUBLANES, :] = jnp.zeros((SUBLANES, width), F32)
        h_carry[...] = jnp.zeros_like(h_carry)

    x = x_ref[...]
    h = _norm_modulate(x, g_ref[...], mod_ref[0:1, :], mod_ref[1:2, :]).astype(BF16)

    xbuf[SUBLANES:SUBLANES + ts, :] = (
        jnp.dot(h, win_ref[...], preferred_element_type=F32) + bin_ref[...])
    xc = cb_ref[...]
    for k in range(CONV_WIDTH):
        start = SUBLANES - (CONV_WIDTH - 1) + k
        xc = xc + xbuf[start:start + ts, :] * cw_ref[k:k + 1, :]
    xbuf[0:SUBLANES, :] = xbuf[ts:ts + SUBLANES, :]

    xcb = xc.astype(BF16)
    log_sig = LRU_C * jax.nn.log_sigmoid(lam_ref[...])
    for hh in range(heads):
        cols = slice(hh * hd, (hh + 1) * hd)
        z = jnp.dot(xcb[:, cols], wax_ref[hh], preferred_element_type=F32) + bax_ref[hh]
        gate_r = jax.nn.sigmoid(z[:, :hd])
        gate_i = jax.nn.sigmoid(z[:, hd:])
        log_a = gate_r * log_sig[:, cols]
        a_buf[:, cols] = jnp.exp(log_a)
        th = jnp.tanh(log_a)
        u_buf[:, cols] = jnp.sqrt(-2.0 * th / (1.0 - th)) * (gate_i * xc[:, cols])

    row = lax.broadcasted_iota(jnp.int32, (SUBLANES, width), 0)

    def block(j, carry):
        r0 = pl.multiple_of(j * SUBLANES, SUBLANES)
        a = a_buf[pl.ds(r0, SUBLANES), :]
        u = u_buf[pl.ds(r0, SUBLANES), :]
        k = 1
        while k < SUBLANES:
            keep = row >= k
            a_prev = jnp.where(keep, pltpu.roll(a, k, 0), 1.0)
            u_prev = jnp.where(keep, pltpu.roll(u, k, 0), 0.0)
            u = u + a * u_prev
            a = a * a_prev
            k *= 2
        hb = u + a * carry
        u_buf[pl.ds(r0, SUBLANES), :] = hb
        return hb[SUBLANES - 1:SUBLANES, :]

    h_carry[...] = lax.fori_loop(0, ts // SUBLANES, block, h_carry[...], unroll=2)

    gate_branch = jax.nn.gelu(jnp.dot(h, wy_ref[...], preferred_element_type=F32) + by_ref[...])
    y = jnp.dot((u_buf[...] * gate_branch).astype(BF16), wout_ref[...],
                preferred_element_type=F32) + bout_ref[...]
    o_ref[...] = x + mod_ref[2:3, :] * y


def _lru_layer(x, mod, g, w_y, b_y, w_in, b_in, conv_w, conv_b, w_a, b_a, w_x, b_x, lam,
               w_out, b_out):
    bsz, seq, d = x.shape
    width = w_y.shape[1]
    heads, hd, _ = w_a.shape
    w_ax = jnp.concatenate([w_a, w_x], axis=-1).astype(BF16)
    b_ax = jnp.concatenate([b_a, b_x], axis=-1).reshape(heads, 1, 2 * hd)
    operands = [x, mod, g.reshape(1, d), w_y.astype(BF16), b_y.reshape(1, width),
                w_in.astype(BF16), b_in.reshape(1, width), conv_w, conv_b.reshape(1, width),
                w_ax, b_ax, lam.reshape(1, width), w_out.astype(BF16), b_out.reshape(1, d)]
    in_specs = [_tile_spec(SEQ_TILE, d), _mod_spec(d)] + [
        _const_spec(op.shape) for op in operands[2:]]
    return pl.pallas_call(
        _lru_kernel,
        grid=(bsz, seq // SEQ_TILE),
        in_specs=in_specs,
        out_specs=_tile_spec(SEQ_TILE, d),
        out_shape=jax.ShapeDtypeStruct(x.shape, F32),
        scratch_shapes=[
            pltpu.VMEM((SEQ_TILE + SUBLANES, width), F32),
            pltpu.VMEM((SEQ_TILE, width), F32),
            pltpu.VMEM((SEQ_TILE, width), F32),
            pltpu.VMEM((1, width), F32),
        ],
        compiler_params=_params(),
        name="lru_mixer",
    )(*operands)


def _pool_kernel(x_ref, mod_ref, g_ref, pw_ref, ps_ref, o_ref, hbuf):
    ts, d = x_ref.shape
    groups, gd, _ = pw_ref.shape
    s = pl.program_id(1)

    @pl.when(s == 0)
    def _():
        hbuf[0:POOL_HALO, :] = jnp.zeros((POOL_HALO, d), F32)

    x = x_ref[...]
    h = _norm_modulate(x, g_ref[...], mod_ref[0:1, :], mod_ref[1:2, :])
    hbuf[POOL_HALO:POOL_HALO + ts, :] = h

    pos = s * ts + lax.broadcasted_iota(jnp.int32, (ts, 1), 0) + 1
    mixed = []
    for gi, win in enumerate(POOL_WINDOWS):
        cols = slice(gi * gd, (gi + 1) * gd)
        hg = h[:, cols]
        total = hg
        for k in range(1, win):
            total = total + hbuf[POOL_HALO - k:POOL_HALO - k + ts, cols]
        inv_count = 1.0 / jnp.minimum(pos, win).astype(F32)
        pooled = total * inv_count - hg
        mixed.append(jnp.dot(pooled.astype(BF16), pw_ref[gi], preferred_element_type=F32))
    hbuf[0:POOL_HALO, :] = hbuf[ts:ts + POOL_HALO, :]
    y = jnp.concatenate(mixed, axis=-1) * ps_ref[...]
    o_ref[...] = x + mod_ref[2:3, :] * y


def _pool_layer(x, mod, g, pool_w, pool_scale):
    bsz, seq, d = x.shape
    operands = [x, mod, g.reshape(1, d), pool_w.astype(BF16), pool_scale.reshape(1, d)]
    in_specs = [_tile_spec(SEQ_TILE, d), _mod_spec(d)] + [
        _const_spec(op.shape) for op in operands[2:]]
    return pl.pallas_call(
        _pool_kernel,
        grid=(bsz, seq // SEQ_TILE),
        in_specs=in_specs,
        out_specs=_tile_spec(SEQ_TILE, d),
        out_shape=jax.ShapeDtypeStruct(x.shape, F32),
        scratch_shapes=[pltpu.VMEM((SEQ_TILE + POOL_HALO, d), F32)],
        compiler_params=_params(),
        name="pool_mixer",
    )(*operands)


def kernel(x, c, w_mod, b_mod, norm_mix_g, norm_ffn_g, lru_w_y, lru_b_y, lru_w_in, lru_b_in,
           lru_conv_w, lru_conv_b, lru_w_a, lru_b_a, lru_w_x, lru_b_x, lru_lambda, lru_w_out,
           lru_b_out, pool_w, pool_scale, ffn_w1, ffn_w2, final_norm_g):
    depth = w_mod.shape[0]
    assert x.shape[1] % SEQ_TILE == 0 and ffn_w1.shape[2] % FF_CHUNK == 0
    assert w_mod.shape[2] % MOD_COLS == 0
    mod = _modulation(c, w_mod, b_mod)
    for i in range(depth):
        j = i // N_MIXERS
        if i % N_MIXERS == 0:
            x = _lru_layer(x, mod[i], norm_mix_g[i], lru_w_y[j], lru_b_y[j], lru_w_in[j],
                           lru_b_in[j], lru_conv_w[j], lru_conv_b[j], lru_w_a[j], lru_b_a[j],
                           lru_w_x[j], lru_b_x[j], lru_lambda[j], lru_w_out[j], lru_b_out[j])
        else:
            x = _pool_layer(x, mod[i], norm_mix_g[i], pool_w[j], pool_scale[j])
        x = _mlp_layer(x, mod[i], norm_ffn_g[i], ffn_w1[i].astype(BF16),
                       ffn_w2[i].astype(BF16),
                       final_norm_g if i == depth - 1 else None)
    return x
```

```python
import functools
import math

import jax
import jax.numpy as jnp
from jax import lax
from jax.experimental import pallas as pl
from jax.experimental.pallas import tpu as pltpu

F32 = jnp.float32
BF16 = jnp.bfloat16

N_MIXERS = 2
N_MOD = 6
CONV_WIDTH = 4
LRU_C = 8.0
POOL_WINDOWS = (2, 4, 8, 16)
EPS = 1e-6

SUBLANES = 8
LANES = 128
SEQ_TILE = 512
SEG_LEN = SEQ_TILE // SUBLANES
SEG_PITCH = SEG_LEN + SUBLANES
FF_CHUNK = 1024
MOD_COLS = 1536
POOL_HALO = max(POOL_WINDOWS)
VMEM_LIMIT = 52 * 1024 * 1024

GELU_C1 = math.sqrt(2.0 / math.pi)
GELU_C2 = GELU_C1 * 0.044715


def _const_spec(shape):
    nd = len(shape)
    return pl.BlockSpec(shape, lambda b, s: (0,) * nd, pipeline_mode=pl.Buffered(1))


def _layer_spec(shape, layer):
    nd = len(shape) - 1
    return pl.BlockSpec((None,) + tuple(shape[1:]), lambda b, s: (layer,) + (0,) * nd,
                        pipeline_mode=pl.Buffered(1))


def _tile_spec(ts, d):
    return pl.BlockSpec((None, ts, d), lambda b, s: (b, s, 0))


def _mod_spec(d, layer):
    return pl.BlockSpec((None, None, N_MOD, d), lambda b, s: (layer, b, 0, 0))


def _params():
    return pltpu.CompilerParams(
        dimension_semantics=("arbitrary", "arbitrary"), vmem_limit_bytes=VMEM_LIMIT)


def _rms_scale(x):
    return lax.rsqrt(jnp.mean(x * x, axis=-1, keepdims=True) + EPS)


def _norm_modulate(x, g, shift, scale):
    return (x * _rms_scale(x)) * (g * (1.0 + scale)) + shift


def _mod_kernel(c_ref, w_ref, b_ref, o_ref):
    c = c_ref[...]
    cond = c * jax.nn.sigmoid(c)
    o_ref[...] = jnp.dot(cond.astype(BF16), w_ref[...].astype(BF16),
                         preferred_element_type=F32) + b_ref[...]


def _modulation(c, w_mod, b_mod):
    depth, d, e = w_mod.shape
    bsz = c.shape[0]
    rows = -(-bsz // SUBLANES) * SUBLANES
    c_pad = jnp.zeros((rows, d), F32).at[:bsz].set(c)
    out = pl.pallas_call(
        _mod_kernel,
        grid=(depth, e // MOD_COLS),
        in_specs=[
            pl.BlockSpec((rows, d), lambda i, j: (0, 0)),
            pl.BlockSpec((None, d, MOD_COLS), lambda i, j: (i, 0, j)),
            pl.BlockSpec((None, 1, MOD_COLS), lambda i, j: (i, 0, j)),
        ],
        out_specs=pl.BlockSpec((None, rows, MOD_COLS), lambda i, j: (i, 0, j)),
        out_shape=jax.ShapeDtypeStruct((depth, rows, e), F32),
        compiler_params=pltpu.CompilerParams(
            dimension_semantics=("arbitrary", "arbitrary"), vmem_limit_bytes=VMEM_LIMIT),
        name="adaln_mod",
    )(c_pad, w_mod, b_mod.reshape(depth, 1, e))
    return out[:, :bsz].reshape(depth, bsz, N_MOD, d)


def _mlp_kernel(*refs, final):
    if final:
        x_ref, mod_ref, g_ref, w1_ref, w2_ref, gf_ref, o_ref = refs
    else:
        x_ref, mod_ref, g_ref, w1_ref, w2_ref, o_ref = refs
    x = x_ref[...]
    h = _norm_modulate(x, g_ref[...], mod_ref[3:4, :], mod_ref[4:5, :]).astype(BF16)
    d_ff = w1_ref.shape[1]
    acc = None
    for k in range(d_ff // FF_CHUNK):
        cols = slice(k * FF_CHUNK, (k + 1) * FF_CHUNK)
        u = jnp.maximum(jnp.dot(h, w1_ref[:, cols], preferred_element_type=F32), 0.0)
        p = jnp.dot((u * u).astype(BF16), w2_ref[cols, :], preferred_element_type=F32)
        acc = p if acc is None else acc + p
    y = x + mod_ref[5:6, :] * acc
    if final:
        y = (y * _rms_scale(y)) * gf_ref[...]
    o_ref[...] = y


def _mlp_layer(x, mod, layer, g, w1, w2, final_g=None):
    bsz, seq, d = x.shape
    final = final_g is not None
    operands = [x, mod, g, w1, w2]
    in_specs = [_tile_spec(SEQ_TILE, d), _mod_spec(d, layer), _layer_spec(g.shape, layer),
                _layer_spec(w1.shape, layer), _layer_spec(w2.shape, layer)]
    if final:
        operands.append(final_g.reshape(1, d))
        in_specs.append(_const_spec((1, d)))
    return pl.pallas_call(
        functools.partial(_mlp_kernel, final=final),
        grid=(bsz, seq // SEQ_TILE),
        in_specs=in_specs,
        out_specs=_tile_spec(SEQ_TILE, d),
        out_shape=jax.ShapeDtypeStruct(x.shape, F32),
        compiler_params=_params(),
        name="mlp_final" if final else "mlp",
    )(*operands)


def _sigmoid_of_twice(v):
    return 0.5 * jnp.tanh(v) + 0.5


def _gelu_tanh(v):
    return (0.5 * v) * (1.0 + jnp.tanh(v * (GELU_C1 + GELU_C2 * (v * v))))


def _lru_kernel(x_ref, mod_ref, g_ref, wy_ref, by_ref, win_ref, bin_ref, cw_ref, cb_ref,
                wax_ref, bax_ref, lam_ref, wout_ref, bout_ref, o_ref,
                h_slab, h_perm, xbuf, a_buf, u_buf, y_slab, h_carry):
    ts, width = a_buf.shape
    d = x_ref.shape[1]
    heads, hd, _ = wax_ref.shape
    halo = (CONV_WIDTH - 1) * SUBLANES

    @pl.when(pl.program_id(1) == 0)
    def _():
        xbuf[ts:ts + halo, :] = jnp.zeros((halo, width), F32)
        h_carry[...] = jnp.zeros_like(h_carry)

    x = x_ref[...]
    h = _norm_modulate(x, g_ref[...], mod_ref[0:1, :], mod_ref[1:2, :])

    for c in range(d // LANES):
        for r in range(SUBLANES):
            h_slab[c, r * SEG_PITCH:r * SEG_PITCH + SEG_LEN, :] = (
                h[r * SEG_LEN:(r + 1) * SEG_LEN, c * LANES:(c + 1) * LANES])

    def permute(j, _):
        r0 = pl.multiple_of(j * SUBLANES, SUBLANES)
        for c in range(d // LANES):
            h_perm[pl.ds(r0, SUBLANES), c * LANES:(c + 1) * LANES] = (
                h_slab[c, pl.ds(j, SUBLANES, stride=SEG_PITCH), :])
        return 0

    lax.fori_loop(0, SEG_LEN, permute, 0, unroll=4)
    hp = h_perm[...].astype(BF16)

    prev_tail = xbuf[ts:ts + halo, :]
    xr = jnp.dot(hp, win_ref[...], preferred_element_type=F32) + bin_ref[...]
    xbuf[halo:halo + ts, :] = xr
    row = lax.broadcasted_iota(jnp.int32, (SUBLANES, width), 0)
    for p in range(CONV_WIDTH - 1):
        blk = slice(p * SUBLANES, (p + 1) * SUBLANES)
        cur = xr[ts - halo + p * SUBLANES:ts - halo + (p + 1) * SUBLANES, :]
        xbuf[blk, :] = jnp.where(row == 0, pltpu.roll(prev_tail[blk, :], 1, 0),
                                 pltpu.roll(cur, 1, 0))
    xc = cb_ref[...]
    for k in range(CONV_WIDTH):
        xc = xc + xbuf[k * SUBLANES:k * SUBLANES + ts, :] * cw_ref[k:k + 1, :]

    log_sig = LRU_C * jax.nn.log_sigmoid(lam_ref[...])
    for hh in range(heads):
        cols = slice(hh * hd, (hh + 1) * hd)
        xch = xc[:, cols]
        z = jnp.dot(xch.astype(BF16), wax_ref[hh], preferred_element_type=F32) + bax_ref[hh]
        gate_r = _sigmoid_of_twice(z[:, :hd])
        gate_i = _sigmoid_of_twice(z[:, hd:])
        log_a = gate_r * log_sig[:, cols]
        a_buf[:, cols] = jnp.exp(log_a)
        t = jnp.tanh(log_a)
        q = -2.0 * t
        mult = jnp.where(q > 0.0, q * lax.rsqrt(q * (1.0 - t)), 0.0)
        u_buf[:, cols] = mult * (gate_i * xch)

    def advance(j, carry):
        h_loc, prod = carry
        r0 = pl.multiple_of(j * SUBLANES, SUBLANES)
        a = a_buf[pl.ds(r0, SUBLANES), :]
        h_loc = a * h_loc + u_buf[pl.ds(r0, SUBLANES), :]
        prod = a * prod
        u_buf[pl.ds(r0, SUBLANES), :] = h_loc
        a_buf[pl.ds(r0, SUBLANES), :] = prod
        return h_loc, prod

    seg_u, seg_a = lax.fori_loop(
        0, SEG_LEN, advance,
        (jnp.zeros((SUBLANES, width), F32), jnp.ones((SUBLANES, width), F32)), unroll=4)
    k = 1
    while k < SUBLANES:
        keep = row >= k
        a_prev = jnp.where(keep, pltpu.roll(seg_a, k, 0), 1.0)
        u_prev = jnp.where(keep, pltpu.roll(seg_u, k, 0), 0.0)
        seg_u = seg_u + seg_a * u_prev
        seg_a = seg_a * a_prev
        k *= 2
    state_in = h_carry[...]
    seg_end = seg_a * state_in + seg_u
    h_carry[...] = seg_end[SUBLANES - 1:SUBLANES, :]
    seg_start = jnp.where(row == 0, state_in, pltpu.roll(seg_end, 1, 0))
    hs = u_buf[...] + a_buf[...] * pltpu.repeat(seg_start, SEG_LEN, 0)

    gate_branch = _gelu_tanh(
        jnp.dot(hp, wy_ref[...], preferred_element_type=F32) + by_ref[...])
    y = jnp.dot((hs * gate_branch).astype(BF16), wout_ref[...],
                preferred_element_type=F32) + bout_ref[...]
    y = mod_ref[2:3, :] * y

    for c in range(d // LANES):
        y_slab[c] = y[:, c * LANES:(c + 1) * LANES]

    def unpermute(n, _):
        q = n // SUBLANES
        r = n % SUBLANES
        src = q * (SUBLANES * SUBLANES) + r
        dst = pl.multiple_of(r * SEG_LEN + q * SUBLANES, SUBLANES)
        for c in range(d // LANES):
            lanes = slice(c * LANES, (c + 1) * LANES)
            o_ref[pl.ds(dst, SUBLANES), lanes] = (
                x_ref[pl.ds(dst, SUBLANES), lanes]
                + y_slab[c, pl.ds(src, SUBLANES, stride=SUBLANES), :])
        return 0

    lax.fori_loop(0, ts // SUBLANES, unpermute, 0, unroll=4)


def _lru_layer(x, mod, layer, j, g, w_y, b_y, w_in, b_in, conv_w, conv_b, w_ax, b_ax, lam,
               w_out, b_out):
    bsz, seq, d = x.shape
    width = w_y.shape[2]
    operands = [x, mod, g, w_y, b_y, w_in, b_in, conv_w, conv_b, w_ax, b_ax, lam, w_out, b_out]
    in_specs = [_tile_spec(SEQ_TILE, d), _mod_spec(d, layer), _layer_spec(g.shape, layer)] + [
        _layer_spec(op.shape, j) for op in operands[3:]]
    return pl.pallas_call(
        _lru_kernel,
        grid=(bsz, seq // SEQ_TILE),
        in_specs=in_specs,
        out_specs=_tile_spec(SEQ_TILE, d),
        out_shape=jax.ShapeDtypeStruct(x.shape, F32),
        scratch_shapes=[
            pltpu.VMEM((d // LANES, SUBLANES * SEG_PITCH, LANES), F32),
            pltpu.VMEM((SEQ_TILE, d), F32),
            pltpu.VMEM((SEQ_TILE + (CONV_WIDTH - 1) * SUBLANES, width), F32),
            pltpu.VMEM((SEQ_TILE, width), F32),
            pltpu.VMEM((SEQ_TILE, width), F32),
            pltpu.VMEM((d // LANES, SEQ_TILE, LANES), F32),
            pltpu.VMEM((1, width), F32),
        ],
        compiler_params=_params(),
        name="lru_mixer",
    )(*operands)


def _pool_kernel(x_ref, mod_ref, g_ref, pw_ref, ps_ref, o_ref, hbuf):
    ts, d = x_ref.shape
    groups, gd, _ = pw_ref.shape
    s = pl.program_id(1)

    @pl.when(s == 0)
    def _():
        hbuf[0:POOL_HALO, :] = jnp.zeros((POOL_HALO, d), F32)

    x = x_ref[...]
    h = _norm_modulate(x, g_ref[...], mod_ref[0:1, :], mod_ref[1:2, :])
    hbuf[POOL_HALO:POOL_HALO + ts, :] = h

    pos = s * ts + lax.broadcasted_iota(jnp.int32, (ts, 1), 0) + 1
    mixed = []
    for gi, win in enumerate(POOL_WINDOWS):
        cols = slice(gi * gd, (gi + 1) * gd)
        hg = h[:, cols]
        total = hg
        for k in range(1, win):
            total = total + hbuf[POOL_HALO - k:POOL_HALO - k + ts, cols]
        inv_count = 1.0 / jnp.minimum(pos, win).astype(F32)
        pooled = total * inv_count - hg
        mixed.append(jnp.dot(pooled.astype(BF16), pw_ref[gi], preferred_element_type=F32))
    hbuf[0:POOL_HALO, :] = hbuf[ts:ts + POOL_HALO, :]
    y = jnp.concatenate(mixed, axis=-1) * ps_ref[...]
    o_ref[...] = x + mod_ref[2:3, :] * y


def _pool_layer(x, mod, layer, j, g, pool_w, pool_scale):
    bsz, seq, d = x.shape
    operands = [x, mod, g, pool_w, pool_scale]
    in_specs = [_tile_spec(SEQ_TILE, d), _mod_spec(d, layer), _layer_spec(g.shape, layer),
                _layer_spec(pool_w.shape, j), _layer_spec(pool_scale.shape, j)]
    return pl.pallas_call(
        _pool_kernel,
        grid=(bsz, seq // SEQ_TILE),
        in_specs=in_specs,
        out_specs=_tile_spec(SEQ_TILE, d),
        out_shape=jax.ShapeDtypeStruct(x.shape, F32),
        scratch_shapes=[pltpu.VMEM((SEQ_TILE + POOL_HALO, d), F32)],
        compiler_params=_params(),
        name="pool_mixer",
    )(*operands)


def _rows(v):
    return v.reshape(v.shape[0], 1, v.shape[1])


def kernel(x, c, w_mod, b_mod, norm_mix_g, norm_ffn_g, lru_w_y, lru_b_y, lru_w_in, lru_b_in,
           lru_conv_w, lru_conv_b, lru_w_a, lru_b_a, lru_w_x, lru_b_x, lru_lambda, lru_w_out,
           lru_b_out, pool_w, pool_scale, ffn_w1, ffn_w2, final_norm_g):
    depth = w_mod.shape[0]
    assert x.shape[1] % SEQ_TILE == 0 and ffn_w1.shape[2] % FF_CHUNK == 0
    assert w_mod.shape[2] % MOD_COLS == 0 and x.shape[2] % LANES == 0
    mod = _modulation(c, w_mod, b_mod)

    n_a, heads, hd, _ = lru_w_a.shape
    w_ax = (0.5 * jnp.concatenate([lru_w_a, lru_w_x], axis=-1)).astype(BF16)
    b_ax = (0.5 * jnp.concatenate([lru_b_a, lru_b_x], axis=-1)).reshape(n_a, heads, 1, 2 * hd)
    lru_ops = (lru_w_y.astype(BF16), _rows(lru_b_y), lru_w_in.astype(BF16), _rows(lru_b_in),
               lru_conv_w, _rows(lru_conv_b), w_ax, b_ax, _rows(lru_lambda),
               lru_w_out.astype(BF16), _rows(lru_b_out))
    pool_ops = (pool_w.astype(BF16), _rows(pool_scale))
    g_mix, g_ffn = _rows(norm_mix_g), _rows(norm_ffn_g)
    w1, w2 = ffn_w1.astype(BF16), ffn_w2.astype(BF16)

    for i in range(depth):
        j = i // N_MIXERS
        if i % N_MIXERS == 0:
            x = _lru_layer(x, mod, i, j, g_mix, *lru_ops)
        else:
            x = _pool_layer(x, mod, i, j, g_mix, *pool_ops)
        x = _mlp_layer(x, mod, i, g_ffn, w1, w2,
                       final_norm_g if i == depth - 1 else None)
    return x
```

```python
import functools
import math

import jax
import jax.numpy as jnp
from jax import lax
from jax.experimental import pallas as pl
from jax.experimental.pallas import tpu as pltpu

F32 = jnp.float32
BF16 = jnp.bfloat16

N_MIXERS = 2
N_MOD = 6
CONV_WIDTH = 4
LRU_C = 8.0
POOL_WINDOWS = (2, 4, 8, 16)
EPS = 1e-6

SUBLANES = 8
LANES = 128
SEQ_TILE = 512
SEG_LEN = SEQ_TILE // SUBLANES
SEG_PITCH = SEG_LEN + SUBLANES
FF_CHUNK = 1024
MOD_COLS = 1536
CONV_HALO = CONV_WIDTH - 1
POOL_HALO = max(POOL_WINDOWS)
VMEM_LIMIT = 58 * 1024 * 1024

GELU_C1 = math.sqrt(2.0 / math.pi)
GELU_C2 = GELU_C1 * 0.044715

LRU_INTERLEAVE = (1, 1, 1, 2, 2, 2, 1, 1)
POOL_INTERLEAVE = (1, 1, 1, 1, 1, 1, 1, 0)


def _rms_scale(x):
    return lax.rsqrt(jnp.mean(x * x, axis=-1, keepdims=True) + EPS)


def _norm_modulate(x, g, shift, scale):
    return (x * _rms_scale(x)) * (g * (1.0 + scale)) + shift


def _mod_kernel(c_ref, w_ref, b_ref, o_ref):
    c = c_ref[...]
    cond = c * jax.nn.sigmoid(c)
    o_ref[...] = jnp.dot(cond.astype(BF16), w_ref[...].astype(BF16),
                         preferred_element_type=F32) + b_ref[...]


def _modulation(c, w_mod, b_mod):
    depth, d, e = w_mod.shape
    bsz = c.shape[0]
    rows = -(-bsz // SUBLANES) * SUBLANES
    c_pad = jnp.zeros((rows, d), F32).at[:bsz].set(c)
    out = pl.pallas_call(
        _mod_kernel,
        grid=(depth, e // MOD_COLS),
        in_specs=[
            pl.BlockSpec((rows, d), lambda i, j: (0, 0)),
            pl.BlockSpec((None, d, MOD_COLS), lambda i, j: (i, 0, j)),
            pl.BlockSpec((None, 1, MOD_COLS), lambda i, j: (i, 0, j)),
        ],
        out_specs=pl.BlockSpec((None, rows, MOD_COLS), lambda i, j: (i, 0, j)),
        out_shape=jax.ShapeDtypeStruct((depth, rows, e), F32),
        compiler_params=pltpu.CompilerParams(
            dimension_semantics=("arbitrary", "arbitrary"), vmem_limit_bytes=VMEM_LIMIT),
        name="adaln_mod",
    )(c_pad, w_mod, b_mod.reshape(depth, 1, e))
    return out[:, :bsz].reshape(depth, bsz, N_MOD, d)


def _to_slab(h, slab):
    for c in range(h.shape[1] // LANES):
        for r in range(SUBLANES):
            slab[c, r * SEG_PITCH:r * SEG_PITCH + SEG_LEN, :] = (
                h[r * SEG_LEN:(r + 1) * SEG_LEN, c * LANES:(c + 1) * LANES])


def _permuted_block(slab, j, c):
    return slab[c, pl.ds(j, SUBLANES, stride=SEG_PITCH), :]


def _shift_segments(prev_block, cur_block, row):
    return jnp.where(row == 0, pltpu.roll(prev_block, 1, 0), pltpu.roll(cur_block, 1, 0))


def _sigmoid_of_twice(v):
    return 0.5 * jnp.tanh(v) + 0.5


def _gelu_tanh(v):
    return (0.5 * v) * (1.0 + jnp.tanh(v * (GELU_C1 + GELU_C2 * (v * v))))


def _lru_reset(scratch):
    h_perm, xbuf, a_buf, u_buf, h_carry = scratch
    ts, width = a_buf.shape
    xbuf[ts:ts + CONV_HALO * SUBLANES, :] = jnp.zeros((CONV_HALO * SUBLANES, width), F32)
    h_carry[...] = jnp.zeros_like(h_carry)


def _lru_phases(h, gate, slab, y_slab, weights, scratch):
    wy_ref, by_ref, win_ref, bin_ref, cw_ref, cb_ref, wax_ref, bax_ref, lam_ref, wout_ref, \
        bout_ref = weights
    h_perm, xbuf, a_buf, u_buf, h_carry = scratch
    ts, width = a_buf.shape
    d = h_perm.shape[1]
    heads, hd, _ = wax_ref.shape
    halo = CONV_HALO * SUBLANES
    row = lax.broadcasted_iota(jnp.int32, (SUBLANES, width), 0)

    _to_slab(h, slab)
    for j in range(SEG_LEN):
        for c in range(d // LANES):
            h_perm[j * SUBLANES:(j + 1) * SUBLANES, c * LANES:(c + 1) * LANES] = (
                _permuted_block(slab, j, c))
    hp = h_perm[...].astype(BF16)
    yield

    prev_tail = xbuf[ts:ts + halo, :]
    xr = jnp.dot(hp, win_ref[...], preferred_element_type=F32) + bin_ref[...]
    xbuf[halo:halo + ts, :] = xr
    for p in range(CONV_HALO):
        blk = slice(p * SUBLANES, (p + 1) * SUBLANES)
        cur = xr[ts - halo + p * SUBLANES:ts - halo + (p + 1) * SUBLANES, :]
        xbuf[blk, :] = _shift_segments(prev_tail[blk, :], cur, row)
    yield
    xc = cb_ref[...]
    for k in range(CONV_WIDTH):
        xc = xc + xbuf[k * SUBLANES:k * SUBLANES + ts, :] * cw_ref[k:k + 1, :]
    yield

    log_sig = LRU_C * jax.nn.log_sigmoid(lam_ref[...])
    for hh in range(heads):
        cols = slice(hh * hd, (hh + 1) * hd)
        xch = xc[:, cols]
        z = jnp.dot(xch.astype(BF16), wax_ref[hh], preferred_element_type=F32) + bax_ref[hh]
        gate_r = _sigmoid_of_twice(z[:, :hd])
        gate_i = _sigmoid_of_twice(z[:, hd:])
        log_a = gate_r * log_sig[:, cols]
        a_buf[:, cols] = jnp.exp(log_a)
        t = jnp.tanh(log_a)
        q = -2.0 * t
        mult = jnp.where(q > 0.0, q * lax.rsqrt(q * (1.0 - t)), 0.0)
        u_buf[:, cols] = mult * (gate_i * xch)
        yield

    seg_u = jnp.zeros((SUBLANES, width), F32)
    seg_a = jnp.ones((SUBLANES, width), F32)
    for j in range(SEG_LEN):
        blk = slice(j * SUBLANES, (j + 1) * SUBLANES)
        a = a_buf[blk, :]
        seg_u = a * seg_u + u_buf[blk, :]
        seg_a = a * seg_a
        u_buf[blk, :] = seg_u
        a_buf[blk, :] = seg_a
    k = 1
    while k < SUBLANES:
        keep = row >= k
        a_prev = jnp.where(keep, pltpu.roll(seg_a, k, 0), 1.0)
        u_prev = jnp.where(keep, pltpu.roll(seg_u, k, 0), 0.0)
        seg_u = seg_u + seg_a * u_prev
        seg_a = seg_a * a_prev
        k *= 2
    state_in = h_carry[...]
    seg_end = seg_a * state_in + seg_u
    h_carry[...] = seg_end[SUBLANES - 1:SUBLANES, :]
    seg_start = jnp.where(row == 0, state_in, pltpu.roll(seg_end, 1, 0))
    yield

    hs = (u_buf[...].reshape(SEG_LEN, SUBLANES, width)
          + a_buf[...].reshape(SEG_LEN, SUBLANES, width) * seg_start[None]
          ).reshape(ts, width)
    gate_branch = _gelu_tanh(
        jnp.dot(hp, wy_ref[...], preferred_element_type=F32) + by_ref[...])
    gated = (hs * gate_branch).astype(BF16)
    yield

    y = gate * (jnp.dot(gated, wout_ref[...], preferred_element_type=F32) + bout_ref[...])
    for c in range(d // LANES):
        y_slab[c] = y[:, c * LANES:(c + 1) * LANES]
    yield


def _pool_reset(scratch):
    (hbuf,) = scratch
    ts = hbuf.shape[0] - POOL_HALO * SUBLANES
    hbuf[ts:, :] = jnp.zeros((POOL_HALO * SUBLANES, hbuf.shape[1]), F32)


def _pool_phases(h, gate, slab, y_slab, weights, scratch, tile_start):
    pw_ref, ps_ref = weights
    (hbuf,) = scratch
    halo = POOL_HALO * SUBLANES
    ts = hbuf.shape[0] - halo
    d = hbuf.shape[1]
    groups, gd, _ = pw_ref.shape
    row = lax.broadcasted_iota(jnp.int32, (SUBLANES, LANES), 0)

    _to_slab(h, slab)
    for j in list(range(SEG_LEN - POOL_HALO, SEG_LEN)) + list(range(SEG_LEN - POOL_HALO)):
        for c in range(d // LANES):
            lanes = slice(c * LANES, (c + 1) * LANES)
            dst = slice(halo + j * SUBLANES, halo + (j + 1) * SUBLANES)
            new = _permuted_block(slab, j, c)
            if j >= SEG_LEN - POOL_HALO:
                p = j - (SEG_LEN - POOL_HALO)
                hbuf[p * SUBLANES:(p + 1) * SUBLANES, lanes] = _shift_segments(
                    hbuf[dst, lanes], new, row)
            hbuf[dst, lanes] = new
    yield

    m = lax.broadcasted_iota(jnp.int32, (ts, 1), 0)
    pos = tile_start + (m % SUBLANES) * SEG_LEN + m // SUBLANES + 1
    mixed = []
    for gi, win in enumerate(POOL_WINDOWS):
        cols = slice(gi * gd, (gi + 1) * gd)
        total = hbuf[(POOL_HALO - (win - 1)) * SUBLANES:, cols]
        step = 1
        while step < win:
            total = total[step * SUBLANES:, :] + total[:-step * SUBLANES, :]
            step *= 2
        inv_count = 1.0 / jnp.minimum(pos, win).astype(F32)
        pooled = total * inv_count - hbuf[halo:, cols]
        mixed.append(jnp.dot(pooled.astype(BF16), pw_ref[gi], preferred_element_type=F32))
        yield

    y = gate * (jnp.concatenate(mixed, axis=-1) * ps_ref[...])
    for c in range(d // LANES):
        y_slab[c] = y[:, c * LANES:(c + 1) * LANES]
    yield


def _layer_kernel(*refs, kind, final, tiles_per_seq, n_mixer_weights, n_mixer_scratch):
    x_ref, modm_ref, modf_ref, gm_ref, gf_ref = refs[:5]
    mixer_weights = refs[5:5 + n_mixer_weights]
    rest = refs[5 + n_mixer_weights:]
    if final:
        w1_ref, w2_ref, fg_ref, o_ref = rest[:4]
        rest = rest[4:]
    else:
        w1_ref, w2_ref, o_ref = rest[:3]
        rest = rest[3:]
    slab, y_slab, x1_buf, h1_buf, h_loc = rest[:5]
    mixer_scratch = rest[5:5 + n_mixer_scratch]
    ts, d = x_ref.shape
    g = pl.program_id(0)

    @pl.when(g == 0)
    def _():
        x1_buf[...] = jnp.zeros_like(x1_buf)
        h1_buf[...] = jnp.zeros_like(h1_buf)

    @pl.when(g % tiles_per_seq == 0)
    def _():
        (_lru_reset if kind == "lru" else _pool_reset)(mixer_scratch)

    x = x_ref[...]
    h = _norm_modulate(x, gm_ref[...], modm_ref[0:1, :], modm_ref[1:2, :])
    gate = modm_ref[2:3, :]
    if kind == "lru":
        mixer = _lru_phases(h, gate, slab, y_slab, mixer_weights, mixer_scratch)
        interleave = LRU_INTERLEAVE
    else:
        mixer = _pool_phases(h, gate, slab, y_slab, mixer_weights, mixer_scratch,
                             (g % tiles_per_seq) * ts)
        interleave = POOL_INTERLEAVE

    def mixer_handoff():
        for q in range(SEG_LEN // SUBLANES):
            for r in range(SUBLANES):
                src = q * SUBLANES * SUBLANES + r
                dst = slice(r * SEG_LEN + q * SUBLANES, r * SEG_LEN + (q + 1) * SUBLANES)
                for c in range(d // LANES):
                    lanes = slice(c * LANES, (c + 1) * LANES)
                    x1_buf[dst, lanes] = (
                        x_ref[dst, lanes]
                        + y_slab[c, pl.ds(src, SUBLANES, stride=SUBLANES), :])
        h1_buf[...] = _norm_modulate(x1_buf[...], gf_ref[...], modm_ref[3:4, :],
                                     modm_ref[4:5, :]).astype(BF16)

    def all_phases():
        yield from mixer
        mixer_handoff()
        yield

    phases = all_phases()

    o_ref[...] = x1_buf[...]
    h_loc[...] = h1_buf[...]
    d_ff = w1_ref.shape[1]
    n_chunks = d_ff // FF_CHUNK
    assert len(interleave) == 2 * n_chunks
    acc = None
    for k in range(n_chunks):
        cols = slice(k * FF_CHUNK, (k + 1) * FF_CHUNK)
        u = jnp.maximum(jnp.dot(h_loc[...], w1_ref[:, cols], preferred_element_type=F32), 0.0)
        u = (u * u).astype(BF16)
        for _ in range(interleave[2 * k]):
            next(phases)
        p = jnp.dot(u, w2_ref[cols, :], preferred_element_type=F32)
        acc = p if acc is None else acc + p
        for _ in range(interleave[2 * k + 1]):
            next(phases)
    assert next(phases, "done") == "done"
    out = o_ref[...] + modf_ref[5:6, :] * acc
    if final:
        out = (out * _rms_scale(out)) * fg_ref[...]
    o_ref[...] = out


def _layer(x, mod, layer, kind, g_mix, g_ffn, mixer_ops, j, w1, w2, final_g=None):
    n_tiles, ts, d = x.shape
    bsz = mod.shape[1]
    tiles_per_seq = n_tiles // bsz
    final = final_g is not None
    last = n_tiles - 1

    def resident(op, idx):
        nd = op.ndim - 1
        return pl.BlockSpec((None,) + op.shape[1:], lambda g: (idx,) + (0,) * nd,
                            pipeline_mode=pl.Buffered(1))

    def mix_tile(g):
        return jnp.minimum(g, last)

    def mlp_tile(g):
        return jnp.maximum(g - 1, 0)

    operands = [x, mod, mod, g_mix, g_ffn, *mixer_ops, w1, w2]
    in_specs = [
        pl.BlockSpec((None, ts, d), lambda g: (mix_tile(g), 0, 0)),
        pl.BlockSpec((None, None, N_MOD, d),
                     lambda g: (layer, mix_tile(g) // tiles_per_seq, 0, 0)),
        pl.BlockSpec((None, None, N_MOD, d),
                     lambda g: (layer, mlp_tile(g) // tiles_per_seq, 0, 0)),
        resident(g_mix, layer), resident(g_ffn, layer),
        *[resident(op, j) for op in mixer_ops],
        resident(w1, layer), resident(w2, layer),
    ]
    if final:
        operands.append(final_g.reshape(1, 1, d))
        in_specs.append(resident(operands[-1], 0))

    scratch = [
        pltpu.VMEM((d // LANES, SUBLANES * SEG_PITCH, LANES), F32),
        pltpu.VMEM((d // LANES, ts, LANES), F32),
        pltpu.VMEM((ts, d), F32),
        pltpu.VMEM((ts, d), BF16),
        pltpu.VMEM((ts, d), BF16),
    ]
    if kind == "lru":
        width = mixer_ops[0].shape[2]
        mixer_scratch = [
            pltpu.VMEM((ts, d), F32),
            pltpu.VMEM((ts + CONV_HALO * SUBLANES, width), F32),
            pltpu.VMEM((ts, width), F32),
            pltpu.VMEM((ts, width), F32),
            pltpu.VMEM((1, width), F32),
        ]
    else:
        mixer_scratch = [pltpu.VMEM((ts + POOL_HALO * SUBLANES, d), F32)]

    return pl.pallas_call(
        functools.partial(_layer_kernel, kind=kind, final=final, tiles_per_seq=tiles_per_seq,
                          n_mixer_weights=len(mixer_ops), n_mixer_scratch=len(mixer_scratch)),
        grid=(n_tiles + 1,),
        in_specs=in_specs,
        out_specs=pl.BlockSpec((None, ts, d), lambda g: (mlp_tile(g), 0, 0)),
        out_shape=jax.ShapeDtypeStruct(x.shape, F32),
        scratch_shapes=scratch + mixer_scratch,
        compiler_params=pltpu.CompilerParams(
            dimension_semantics=("arbitrary",), vmem_limit_bytes=VMEM_LIMIT),
        name=kind + ("_layer_final" if final else "_layer"),
    )(*operands)


def _rows(v):
    return v.reshape(v.shape[0], 1, v.shape[1])


def kernel(x, c, w_mod, b_mod, norm_mix_g, norm_ffn_g, lru_w_y, lru_b_y, lru_w_in, lru_b_in,
           lru_conv_w, lru_conv_b, lru_w_a, lru_b_a, lru_w_x, lru_b_x, lru_lambda, lru_w_out,
           lru_b_out, pool_w, pool_scale, ffn_w1, ffn_w2, final_norm_g):
    depth = w_mod.shape[0]
    bsz, seq, d = x.shape
    assert seq % SEQ_TILE == 0 and ffn_w1.shape[2] % FF_CHUNK == 0
    assert w_mod.shape[2] % MOD_COLS == 0 and d % LANES == 0
    mod = _modulation(c, w_mod, b_mod)

    n_a, heads, hd, _ = lru_w_a.shape
    w_ax = (0.5 * jnp.concatenate([lru_w_a, lru_w_x], axis=-1)).astype(BF16)
    b_ax = (0.5 * jnp.concatenate([lru_b_a, lru_b_x], axis=-1)).reshape(n_a, heads, 1, 2 * hd)
    lru_ops = (lru_w_y.astype(BF16), _rows(lru_b_y), lru_w_in.astype(BF16), _rows(lru_b_in),
               lru_conv_w, _rows(lru_conv_b), w_ax, b_ax, _rows(lru_lambda),
               lru_w_out.astype(BF16), _rows(lru_b_out))
    pool_ops = (pool_w.astype(BF16), _rows(pool_scale))
    g_mix, g_ffn = _rows(norm_mix_g), _rows(norm_ffn_g)
    w1, w2 = ffn_w1.astype(BF16), ffn_w2.astype(BF16)

    x = x.reshape(bsz * seq // SEQ_TILE, SEQ_TILE, d)
    for i in range(depth):
        j = i // N_MIXERS
        kind, ops = ("lru", lru_ops) if i % N_MIXERS == 0 else ("pool", pool_ops)
        x = _layer(x, mod, i, kind, g_mix, g_ffn, ops, j, w1, w2,
                   final_norm_g if i == depth - 1 else None)
    return x.reshape(bsz, seq, d)
```

```python
import functools
import math

import jax
import jax.numpy as jnp
from jax import lax
from jax.experimental import pallas as pl
from jax.experimental.pallas import tpu as pltpu

F32 = jnp.float32
BF16 = jnp.bfloat16

N_MIXERS = 2
N_MOD = 6
CONV_WIDTH = 4
LRU_C = 8.0
POOL_WINDOWS = (2, 4, 8, 16)
EPS = 1e-6

SUBLANES = 8
LANES = 128
SEQ_TILE = 512
SEG_LEN = SEQ_TILE // SUBLANES
SEG_PITCH = SEG_LEN + SUBLANES
FF_CHUNK = 512
MOD_COLS = 1536
CONV_HALO = CONV_WIDTH - 1
POOL_HALO = max(POOL_WINDOWS)
VMEM_LIMIT = 58 * 1024 * 1024

GELU_C1 = math.sqrt(2.0 / math.pi)
GELU_C2 = GELU_C1 * 0.044715


def _rms_scale(x):
    return lax.rsqrt(jnp.mean(x * x, axis=-1, keepdims=True) + EPS)


def _splits(n, parts):
    step = n // parts
    return tuple(slice(i * step, (i + 1) * step) for i in range(parts))


def _mod_kernel(c_ref, w_ref, b_ref, o_ref):
    c = c_ref[...]
    cond = c * jax.nn.sigmoid(c)
    o_ref[...] = jnp.dot(cond.astype(BF16), w_ref[...].astype(BF16),
                         preferred_element_type=F32) + b_ref[...]


def _modulation(c, w_mod, b_mod):
    depth, d, e = w_mod.shape
    bsz = c.shape[0]
    rows = -(-bsz // SUBLANES) * SUBLANES
    c_pad = jnp.zeros((rows, d), F32).at[:bsz].set(c)
    out = pl.pallas_call(
        _mod_kernel,
        grid=(depth, e // MOD_COLS),
        in_specs=[
            pl.BlockSpec((rows, d), lambda i, j: (0, 0)),
            pl.BlockSpec((None, d, MOD_COLS), lambda i, j: (i, 0, j)),
            pl.BlockSpec((None, 1, MOD_COLS), lambda i, j: (i, 0, j)),
        ],
        out_specs=pl.BlockSpec((None, rows, MOD_COLS), lambda i, j: (i, 0, j)),
        out_shape=jax.ShapeDtypeStruct((depth, rows, e), F32),
        compiler_params=pltpu.CompilerParams(
            dimension_semantics=("arbitrary", "arbitrary"), vmem_limit_bytes=VMEM_LIMIT),
        name="adaln_mod",
    )(c_pad, w_mod, b_mod.reshape(depth, 1, e))
    return out[:, :bsz].reshape(depth, bsz, N_MOD, d)


def _norm_to_slab(x_ref, g, shift, scale, slab):
    d = x_ref.shape[1]
    r_scale = _rms_scale(x_ref[...])
    gs = g * (1.0 + scale)
    yield 256, ()
    for cols in _splits(d, 4):
        h = (x_ref[:, cols] * r_scale) * gs[:, cols] + shift[:, cols]
        for c in range(cols.start // LANES, cols.stop // LANES):
            for r in range(SUBLANES):
                slab[c, r * SEG_PITCH:r * SEG_PITCH + SEG_LEN, :] = (
                    h[r * SEG_LEN:(r + 1) * SEG_LEN,
                      c * LANES - cols.start:(c + 1) * LANES - cols.start])
        yield 100, (slab,)


def _permuted_block(slab, j, c):
    return slab[c, pl.ds(j, SUBLANES, stride=SEG_PITCH), :]


def _shift_segments(prev_block, cur_block, row):
    return jnp.where(row == 0, pltpu.roll(prev_block, 1, 0), pltpu.roll(cur_block, 1, 0))


def _store_slab(y, slab, cols):
    for c in range(cols.start // LANES, cols.stop // LANES):
        slab[c, 0:y.shape[0], :] = y[:, c * LANES - cols.start:(c + 1) * LANES - cols.start]


def _sigmoid_of_twice(v):
    return 0.5 * jnp.tanh(v) + 0.5


def _gelu_tanh(v):
    return (0.5 * v) * (1.0 + jnp.tanh(v * (GELU_C1 + GELU_C2 * (v * v))))


def _lru_reset(scratch):
    hp_buf, xbuf, xc_buf, mm_buf, a_buf, u_buf, start_buf, h_carry = scratch
    ts, width = a_buf.shape
    xbuf[ts:ts + CONV_HALO * SUBLANES, :] = jnp.zeros((CONV_HALO * SUBLANES, width), F32)
    h_carry[...] = jnp.zeros_like(h_carry)


def _lru_phases(gate, slab, weights, scratch):
    wy_ref, by_ref, win_ref, bin_ref, cw_ref, cb_ref, wax_ref, bax_ref, lam_ref, wout_ref, \
        bout_ref = weights
    hp_buf, xbuf, xc_buf, mm_buf, a_buf, u_buf, start_buf, h_carry = scratch
    ts, width = a_buf.shape
    d = hp_buf.shape[1]
    heads, hd, _ = wax_ref.shape
    halo = CONV_HALO * SUBLANES
    pair = 2 * SUBLANES

    for js in _splits(SEG_LEN, 2):
        for j in range(js.start, js.stop, 2):
            for c in range(d // LANES):
                both = jnp.concatenate(
                    [_permuted_block(slab, j, c), _permuted_block(slab, j + 1, c)], axis=0)
                hp_buf[j * SUBLANES:j * SUBLANES + pair, c * LANES:(c + 1) * LANES] = (
                    both.astype(BF16))
        yield 100, (hp_buf,)

    for cols in _splits(width, 2):
        n = cols.stop - cols.start
        row = lax.broadcasted_iota(jnp.int32, (SUBLANES, n), 0)
        prev_tail = xbuf[ts:ts + halo, cols]
        xr = jnp.dot(hp_buf[...], win_ref[:, cols], preferred_element_type=F32) + bin_ref[:, cols]
        xbuf[halo:halo + ts, cols] = xr
        for p in range(CONV_HALO):
            blk = slice(p * SUBLANES, (p + 1) * SUBLANES)
            cur = xr[ts - halo + p * SUBLANES:ts - halo + (p + 1) * SUBLANES, :]
            xbuf[blk, cols] = _shift_segments(prev_tail[blk, :], cur, row)
        yield 0, (xbuf,)
    for cols in _splits(width, 4):
        part = cb_ref[:, cols]
        for k in range(CONV_WIDTH):
            part = part + xbuf[k * SUBLANES:k * SUBLANES + ts, cols] * cw_ref[k:k + 1, cols]
        xc_buf[:, cols] = part
        yield 256, (xc_buf,)

    log_sig = LRU_C * jax.nn.log_sigmoid(lam_ref[...])

    def gate_matmul(hh):
        park = slice((hh % 2) * 2 * hd, (hh % 2 + 1) * 2 * hd)
        mm_buf[:, park] = jnp.dot(xc_buf[:, hh * hd:(hh + 1) * hd].astype(BF16), wax_ref[hh],
                                  preferred_element_type=F32) + bax_ref[hh]

    def gate_math(hh, sub, rows):
        base = (hh % 2) * 2 * hd
        cols = slice(hh * hd + sub.start, hh * hd + sub.stop)
        gate_r = _sigmoid_of_twice(mm_buf[rows, base + sub.start:base + sub.stop])
        gate_i = _sigmoid_of_twice(mm_buf[rows, base + hd + sub.start:base + hd + sub.stop])
        log_a = gate_r * log_sig[:, cols]
        a_buf[rows, cols] = jnp.exp(log_a)
        t = jnp.tanh(log_a)
        q = -2.0 * t
        mult = jnp.where(q > 0.0, q * lax.rsqrt(q * (1.0 - t)), 0.0)
        u_buf[rows, cols] = mult * (gate_i * xc_buf[rows, cols])

    for first in range(0, heads, 2):
        for hh in range(first, min(first + 2, heads)):
            gate_matmul(hh)
            yield 0, (mm_buf,)
        for hh in range(first, min(first + 2, heads)):
            for sub in _splits(hd, hd // LANES):
                for rows in _splits(ts, 2):
                    gate_math(hh, sub, rows)
                    yield 240, (a_buf, u_buf)

    for cols in _splits(width, 2):
        n = cols.stop - cols.start
        row = lax.broadcasted_iota(jnp.int32, (SUBLANES, n), 0)
        seg_u = jnp.zeros((SUBLANES, n), F32)
        seg_a = jnp.ones((SUBLANES, n), F32)
        for j in range(SEG_LEN):
            blk = slice(j * SUBLANES, (j + 1) * SUBLANES)
            a = a_buf[blk, cols]
            seg_u = a * seg_u + u_buf[blk, cols]
            seg_a = a * seg_a
            u_buf[blk, cols] = seg_u
            a_buf[blk, cols] = seg_a
        k = 1
        while k < SUBLANES:
            keep = row >= k
            a_prev = jnp.where(keep, pltpu.roll(seg_a, k, 0), 1.0)
            u_prev = jnp.where(keep, pltpu.roll(seg_u, k, 0), 0.0)
            seg_u = seg_u + seg_a * u_prev
            seg_a = seg_a * a_prev
            k *= 2
        state_in = h_carry[:, cols]
        seg_end = seg_a * state_in + seg_u
        h_carry[:, cols] = seg_end[SUBLANES - 1:SUBLANES, :]
        start_buf[:, cols] = jnp.where(row == 0, state_in, pltpu.roll(seg_end, 1, 0))
        yield 200, (a_buf, u_buf)

    for cols in _splits(width, 2):
        mm_buf[:, cols] = (jnp.dot(hp_buf[...], wy_ref[:, cols], preferred_element_type=F32)
                           + by_ref[:, cols])
        yield 0, (mm_buf,)
    for cols in _splits(width, width // LANES):
        n = cols.stop - cols.start
        hs = (u_buf[:, cols].reshape(SEG_LEN, SUBLANES, n)
              + a_buf[:, cols].reshape(SEG_LEN, SUBLANES, n) * start_buf[:, cols][None]
              ).reshape(ts, n)
        hp_buf[:, cols] = (hs * _gelu_tanh(mm_buf[:, cols])).astype(BF16)
        yield 180, (hp_buf,)

    for cols in _splits(d, 2):
        y = gate[:, cols] * (jnp.dot(hp_buf[...], wout_ref[:, cols],
                                     preferred_element_type=F32) + bout_ref[:, cols])
        _store_slab(y, slab, cols)
        yield 0, (slab,)


def _pool_reset(scratch):
    (hbuf,) = scratch
    ts = hbuf.shape[0] - POOL_HALO * SUBLANES
    hbuf[ts:, :] = jnp.zeros((POOL_HALO * SUBLANES, hbuf.shape[1]), F32)


def _pool_phases(gate, slab, weights, scratch, tile_start):
    pw_ref, ps_ref = weights
    (hbuf,) = scratch
    halo = POOL_HALO * SUBLANES
    ts = hbuf.shape[0] - halo
    d = hbuf.shape[1]
    groups, gd, _ = pw_ref.shape
    row = lax.broadcasted_iota(jnp.int32, (SUBLANES, LANES), 0)

    order = list(range(SEG_LEN - POOL_HALO, SEG_LEN)) + list(range(SEG_LEN - POOL_HALO))
    for js in _splits(SEG_LEN, 2):
        for j in order[js]:
            for c in range(d // LANES):
                lanes = slice(c * LANES, (c + 1) * LANES)
                dst = slice(halo + j * SUBLANES, halo + (j + 1) * SUBLANES)
                new = _permuted_block(slab, j, c)
                if j >= SEG_LEN - POOL_HALO:
                    p = j - (SEG_LEN - POOL_HALO)
                    hbuf[p * SUBLANES:(p + 1) * SUBLANES, lanes] = _shift_segments(
                        hbuf[dst, lanes], new, row)
                hbuf[dst, lanes] = new
        yield 100, (hbuf,)

    m = lax.broadcasted_iota(jnp.int32, (ts, 1), 0)
    pos = tile_start + (m % SUBLANES) * SEG_LEN + m // SUBLANES + 1
    for gi, win in enumerate(POOL_WINDOWS):
        cols = slice(gi * gd, (gi + 1) * gd)
        total = hbuf[(POOL_HALO - (win - 1)) * SUBLANES:, cols]
        step = 1
        while step < win:
            total = total[step * SUBLANES:, :] + total[:-step * SUBLANES, :]
            step *= 2
        inv_count = 1.0 / jnp.minimum(pos, win).astype(F32)
        pooled = total * inv_count - hbuf[halo:, cols]
        mixed = jnp.dot(pooled.astype(BF16), pw_ref[gi], preferred_element_type=F32)
        _store_slab(gate[:, cols] * (mixed * ps_ref[:, cols]), slab, cols)
        yield 250, (slab,)


def _lru_costs(heads, hd, width):
    per_pair = [0, 0] + [240] * (2 * 2 * (hd // LANES))
    return ([256] + [100] * 4 + [100] * 2 + [0] * 2 + [256] * 4 + per_pair * (heads // 2)
            + [200] * 2 + [0] * 2 + [180] * (width // LANES) + [0] * 2)


def _pool_costs(groups):
    return [256] + [100] * 4 + [100] * 2 + [250] * groups


HAND_OVER_COSTS = [130, 130, 190, 190, 190, 190]


def _layer_kernel(*refs, kind, final, tiles_per_seq, n_mixer_weights, n_mixer_scratch):
    x_ref, modm_ref, modf_ref, gm_ref, gf_ref = refs[:5]
    mixer_weights = refs[5:5 + n_mixer_weights]
    rest = refs[5 + n_mixer_weights:]
    if final:
        w1_ref, w2_ref, fg_ref, o_ref = rest[:4]
        rest = rest[4:]
    else:
        w1_ref, w2_ref, o_ref = rest[:3]
        rest = rest[3:]
    slab, x1_buf, h1_buf, h_loc = rest[:4]
    mixer_scratch = rest[4:4 + n_mixer_scratch]
    ts, d = x_ref.shape
    g = pl.program_id(0)

    @pl.when(g == 0)
    def _():
        x1_buf[...] = jnp.zeros_like(x1_buf)
        h1_buf[...] = jnp.zeros_like(h1_buf)

    @pl.when(g % tiles_per_seq == 0)
    def _():
        (_lru_reset if kind == "lru" else _pool_reset)(mixer_scratch)

    gate = modm_ref[2:3, :]
    if kind == "lru":
        heads, hd, _ = mixer_weights[6].shape
        assert heads % 2 == 0 and 4 * hd <= d
        mixer = _lru_phases(gate, slab, mixer_weights, mixer_scratch)
        costs = _lru_costs(heads, hd, d)
    else:
        mixer = _pool_phases(gate, slab, mixer_weights, mixer_scratch,
                             (g % tiles_per_seq) * ts)
        costs = _pool_costs(mixer_weights[0].shape[0])
    costs = costs + HAND_OVER_COSTS

    def hand_over(qs):
        for q in qs:
            for r in range(SUBLANES):
                src = q * SUBLANES * SUBLANES + r
                dst = slice(r * SEG_LEN + q * SUBLANES, r * SEG_LEN + (q + 1) * SUBLANES)
                for c in range(d // LANES):
                    lanes = slice(c * LANES, (c + 1) * LANES)
                    x1_buf[dst, lanes] = (
                        x_ref[dst, lanes]
                        + slab[c, pl.ds(src, SUBLANES, stride=SUBLANES), :])

    def all_phases():
        yield from _norm_to_slab(x_ref, gm_ref[...], modm_ref[0:1, :], modm_ref[1:2, :], slab)
        yield from mixer
        for qs in _splits(SEG_LEN // SUBLANES, 2):
            hand_over(range(qs.start, qs.stop))
            yield 130, (x1_buf,)
        gs = gf_ref[...] * (1.0 + modm_ref[4:5, :])
        for rows in _splits(ts, 4):
            x1 = x1_buf[rows, :]
            h1_buf[rows, :] = ((x1 * _rms_scale(x1)) * gs + modm_ref[3:4, :]).astype(BF16)
            yield 190, (h1_buf,)

    phases = all_phases()
    total_cost = float(sum(costs) - sum(HAND_OVER_COSTS))
    spent = 0.0
    n_emitted = 0
    written = []

    def fill(fraction):
        nonlocal spent, n_emitted
        while n_emitted < len(costs) and spent < fraction * total_cost:
            cost, refs = next(phases)
            assert cost == costs[n_emitted], (n_emitted, cost, costs[n_emitted])
            spent += cost
            n_emitted += 1
            written.extend(r for r in refs if all(r is not w for w in written))

    def anchor(target):
        pair = 2 * SUBLANES
        rows = pl.ds(pl.multiple_of(lax.shift_right_logical(g, 30) * pair, pair), pair)
        bits = None
        for ref in written:
            tok = ref[0, rows, :] if len(ref.shape) == 3 else ref[rows, 0:LANES]
            b = pltpu.bitcast(tok, jnp.uint32)
            if b.shape[0] == pair:
                b = b[:SUBLANES] | b[SUBLANES:]
            bits = b if bits is None else bits | b
        del written[:]
        if bits is not None:
            zero = pltpu.bitcast(lax.shift_right_logical(bits, jnp.uint32(32)), BF16)
            target[0:pair, 0:LANES] = target[0:pair, 0:LANES] + zero

    d_ff = w1_ref.shape[1]
    n_chunks = d_ff // FF_CHUNK
    n_windows = n_chunks - 1

    def up(k):
        h_in = h1_buf[...] if k == 0 else h_loc[...]
        cols = slice(k * FF_CHUNK, (k + 1) * FF_CHUNK)
        u = jnp.maximum(jnp.dot(h_in, w1_ref[:, cols], preferred_element_type=F32), 0.0)
        return (u * u).astype(BF16)

    u_next = up(0)
    o_ref[...] = x1_buf[...]
    h_loc[...] = h1_buf[...]
    acc = None
    for k in range(n_chunks):
        u = u_next
        if k + 1 < n_chunks:
            fill((k + 1) / n_windows)
            anchor(h_loc)
            u_next = up(k + 1)
        if k == n_chunks - 2:
            fill(2.0)
        p = jnp.dot(u, w2_ref[k * FF_CHUNK:(k + 1) * FF_CHUNK, :], preferred_element_type=F32)
        acc = p if acc is None else acc + p
    assert n_emitted == len(costs) and next(phases, "done") == "done"
    out = o_ref[...] + modf_ref[5:6, :] * acc
    if final:
        out = (out * _rms_scale(out)) * fg_ref[...]
    o_ref[...] = out


def _layer(x, mod, layer, kind, g_mix, g_ffn, mixer_ops, j, w1, w2, final_g=None):
    n_tiles, ts, d = x.shape
    bsz = mod.shape[1]
    tiles_per_seq = n_tiles // bsz
    final = final_g is not None
    last = n_tiles - 1

    def resident(op, idx):
        nd = op.ndim - 1
        return pl.BlockSpec((None,) + op.shape[1:], lambda g: (idx,) + (0,) * nd,
                            pipeline_mode=pl.Buffered(1))

    def mix_tile(g):
        return jnp.minimum(g, last)

    def mlp_tile(g):
        return jnp.maximum(g - 1, 0)

    operands = [x, mod, mod, g_mix, g_ffn, *mixer_ops, w1, w2]
    in_specs = [
        pl.BlockSpec((None, ts, d), lambda g: (mix_tile(g), 0, 0)),
        pl.BlockSpec((None, None, N_MOD, d),
                     lambda g: (layer, mix_tile(g) // tiles_per_seq, 0, 0)),
        pl.BlockSpec((None, None, N_MOD, d),
                     lambda g: (layer, mlp_tile(g) // tiles_per_seq, 0, 0)),
        resident(g_mix, layer), resident(g_ffn, layer),
        *[resident(op, j) for op in mixer_ops],
        resident(w1, layer), resident(w2, layer),
    ]
    if final:
        operands.append(final_g.reshape(1, 1, d))
        in_specs.append(resident(operands[-1], 0))

    scratch = [
        pltpu.VMEM((d // LANES, SUBLANES * SEG_PITCH, LANES), F32),
        pltpu.VMEM((ts, d), F32),
        pltpu.VMEM((ts, d), BF16),
        pltpu.VMEM((ts, d), BF16),
    ]
    if kind == "lru":
        width = mixer_ops[0].shape[2]
        assert width == d
        mixer_scratch = [
            pltpu.VMEM((ts, d), BF16),
            pltpu.VMEM((ts + CONV_HALO * SUBLANES, width), F32),
            pltpu.VMEM((ts, width), F32),
            pltpu.VMEM((ts, width), F32),
            pltpu.VMEM((ts, width), F32),
            pltpu.VMEM((ts, width), F32),
            pltpu.VMEM((SUBLANES, width), F32),
            pltpu.VMEM((1, width), F32),
        ]
    else:
        mixer_scratch = [pltpu.VMEM((ts + POOL_HALO * SUBLANES, d), F32)]

    return pl.pallas_call(
        functools.partial(_layer_kernel, kind=kind, final=final, tiles_per_seq=tiles_per_seq,
                          n_mixer_weights=len(mixer_ops), n_mixer_scratch=len(mixer_scratch)),
        grid=(n_tiles + 1,),
        in_specs=in_specs,
        out_specs=pl.BlockSpec((None, ts, d), lambda g: (mlp_tile(g), 0, 0)),
        out_shape=jax.ShapeDtypeStruct(x.shape, F32),
        scratch_shapes=scratch + mixer_scratch,
        compiler_params=pltpu.CompilerParams(
            dimension_semantics=("arbitrary",), vmem_limit_bytes=VMEM_LIMIT),
        name=kind + ("_layer_final" if final else "_layer"),
    )(*operands)


def _rows(v):
    return v.reshape(v.shape[0], 1, v.shape[1])


def kernel(x, c, w_mod, b_mod, norm_mix_g, norm_ffn_g, lru_w_y, lru_b_y, lru_w_in, lru_b_in,
           lru_conv_w, lru_conv_b, lru_w_a, lru_b_a, lru_w_x, lru_b_x, lru_lambda, lru_w_out,
           lru_b_out, pool_w, pool_scale, ffn_w1, ffn_w2, final_norm_g):
    depth = w_mod.shape[0]
    bsz, seq, d = x.shape
    assert seq % SEQ_TILE == 0 and ffn_w1.shape[2] % FF_CHUNK == 0
    assert w_mod.shape[2] % MOD_COLS == 0 and d % (4 * LANES) == 0
    mod = _modulation(c, w_mod, b_mod)

    n_a, heads, hd, _ = lru_w_a.shape
    w_ax = (0.5 * jnp.concatenate([lru_w_a, lru_w_x], axis=-1)).astype(BF16)
    b_ax = (0.5 * jnp.concatenate([lru_b_a, lru_b_x], axis=-1)).reshape(n_a, heads, 1, 2 * hd)
    lru_ops = (lru_w_y.astype(BF16), _rows(lru_b_y), lru_w_in.astype(BF16), _rows(lru_b_in),
               lru_conv_w, _rows(lru_conv_b), w_ax, b_ax, _rows(lru_lambda),
               lru_w_out.astype(BF16), _rows(lru_b_out))
    pool_ops = (pool_w.astype(BF16), _rows(pool_scale))
    g_mix, g_ffn = _rows(norm_mix_g), _rows(norm_ffn_g)
    w1, w2 = ffn_w1.astype(BF16), ffn_w2.astype(BF16)

    x = x.reshape(bsz * seq // SEQ_TILE, SEQ_TILE, d)
    for i in range(depth):
        j = i // N_MIXERS
        kind, ops = ("lru", lru_ops) if i % N_MIXERS == 0 else ("pool", pool_ops)
        x = _layer(x, mod, i, kind, g_mix, g_ffn, ops, j, w1, w2,
                   final_norm_g if i == depth - 1 else None)
    return x.reshape(bsz, seq, d)
```

```python
import functools
import math

import jax
import jax.numpy as jnp
from jax import lax
from jax.experimental import pallas as pl
from jax.experimental.pallas import tpu as pltpu

F32 = jnp.float32
BF16 = jnp.bfloat16

N_MIXERS = 2
N_MOD = 6
CONV_WIDTH = 4
LRU_C = 8.0
POOL_WINDOWS = (2, 4, 8, 16)
EPS = 1e-6

SUBLANES = 8
LANES = 128
SEQ_TILE = 512
SEG_LEN = SEQ_TILE // SUBLANES
SEG_PITCH = SEG_LEN + SUBLANES
FF_CHUNK = 1024
MOD_COLS = 1536
CONV_HALO = CONV_WIDTH - 1
POOL_HALO = max(POOL_WINDOWS)
VMEM_LIMIT = 58 * 1024 * 1024

GELU_C1 = math.sqrt(2.0 / math.pi)
GELU_C2 = GELU_C1 * 0.044715


def _rms_scale(x):
    return lax.rsqrt(jnp.mean(x * x, axis=-1, keepdims=True) + EPS)


def _splits(n, parts):
    step = n // parts
    return tuple(slice(i * step, (i + 1) * step) for i in range(parts))


def _mod_kernel(c_ref, w_ref, b_ref, o_ref):
    c = c_ref[...]
    cond = c * jax.nn.sigmoid(c)
    o_ref[...] = jnp.dot(cond.astype(BF16), w_ref[...].astype(BF16),
                         preferred_element_type=F32) + b_ref[...]


def _modulation(c, w_mod, b_mod):
    depth, d, e = w_mod.shape
    bsz = c.shape[0]
    rows = -(-bsz // SUBLANES) * SUBLANES
    c_pad = jnp.zeros((rows, d), F32).at[:bsz].set(c)
    out = pl.pallas_call(
        _mod_kernel,
        grid=(depth, e // MOD_COLS),
        in_specs=[
            pl.BlockSpec((rows, d), lambda i, j: (0, 0)),
            pl.BlockSpec((None, d, MOD_COLS), lambda i, j: (i, 0, j)),
            pl.BlockSpec((None, 1, MOD_COLS), lambda i, j: (i, 0, j)),
        ],
        out_specs=pl.BlockSpec((None, rows, MOD_COLS), lambda i, j: (i, 0, j)),
        out_shape=jax.ShapeDtypeStruct((depth, rows, e), F32),
        compiler_params=pltpu.CompilerParams(
            dimension_semantics=("arbitrary", "arbitrary"), vmem_limit_bytes=VMEM_LIMIT),
        name="adaln_mod",
    )(c_pad, w_mod, b_mod.reshape(depth, 1, e))
    return out[:, :bsz].reshape(depth, bsz, N_MOD, d)


def _norm_to_slab(x_ref, g, shift, scale, slab):
    d = x_ref.shape[1]
    r_scale = _rms_scale(x_ref[...])
    gs = g * (1.0 + scale)
    yield 256, ()
    for cols in _splits(d, 4):
        h = (x_ref[:, cols] * r_scale) * gs[:, cols] + shift[:, cols]
        for c in range(cols.start // LANES, cols.stop // LANES):
            for r in range(SUBLANES):
                slab[c, r * SEG_PITCH:r * SEG_PITCH + SEG_LEN, :] = (
                    h[r * SEG_LEN:(r + 1) * SEG_LEN,
                      c * LANES - cols.start:(c + 1) * LANES - cols.start])
        yield 100, (slab,)


def _permuted_block(slab, j, c):
    return slab[c, pl.ds(j, SUBLANES, stride=SEG_PITCH), :]


def _shift_segments(prev_block, cur_block, row):
    return jnp.where(row == 0, pltpu.roll(prev_block, 1, 0), pltpu.roll(cur_block, 1, 0))


def _store_slab(y, slab, cols):
    for c in range(cols.start // LANES, cols.stop // LANES):
        slab[c, 0:y.shape[0], :] = y[:, c * LANES - cols.start:(c + 1) * LANES - cols.start]


def _sigmoid_of_twice(v):
    return 0.5 * jnp.tanh(v) + 0.5


def _gelu_tanh(v):
    return (0.5 * v) * (1.0 + jnp.tanh(v * (GELU_C1 + GELU_C2 * (v * v))))


def _lru_reset(scratch):
    hp_buf, xbuf, xc_buf, mm_buf, a_buf, u_buf, start_buf, h_carry = scratch
    ts, width = a_buf.shape
    xbuf[ts:ts + CONV_HALO * SUBLANES, :] = jnp.zeros((CONV_HALO * SUBLANES, width), F32)
    h_carry[...] = jnp.zeros_like(h_carry)


def _lru_phases(gate, slab, weights, scratch):
    wy_ref, by_ref, win_ref, bin_ref, cw_ref, cb_ref, wax_ref, bax_ref, lam_ref, wout_ref, \
        bout_ref = weights
    hp_buf, xbuf, xc_buf, mm_buf, a_buf, u_buf, start_buf, h_carry = scratch
    ts, width = a_buf.shape
    d = hp_buf.shape[1]
    heads, hd, _ = wax_ref.shape
    halo = CONV_HALO * SUBLANES
    pair = 2 * SUBLANES

    for js in _splits(SEG_LEN, 2):
        for j in range(js.start, js.stop, 2):
            for c in range(d // LANES):
                both = jnp.concatenate(
                    [_permuted_block(slab, j, c), _permuted_block(slab, j + 1, c)], axis=0)
                hp_buf[j * SUBLANES:j * SUBLANES + pair, c * LANES:(c + 1) * LANES] = (
                    both.astype(BF16))
        yield 100, (hp_buf,)

    for cols in _splits(width, 2):
        n = cols.stop - cols.start
        row = lax.broadcasted_iota(jnp.int32, (SUBLANES, n), 0)
        prev_tail = xbuf[ts:ts + halo, cols]
        xr = jnp.dot(hp_buf[...], win_ref[:, cols], preferred_element_type=F32) + bin_ref[:, cols]
        xbuf[halo:halo + ts, cols] = xr
        for p in range(CONV_HALO):
            blk = slice(p * SUBLANES, (p + 1) * SUBLANES)
            cur = xr[ts - halo + p * SUBLANES:ts - halo + (p + 1) * SUBLANES, :]
            xbuf[blk, cols] = _shift_segments(prev_tail[blk, :], cur, row)
        yield 0, (xbuf,)
    for cols in _splits(width, 4):
        part = cb_ref[:, cols]
        for k in range(CONV_WIDTH):
            part = part + xbuf[k * SUBLANES:k * SUBLANES + ts, cols] * cw_ref[k:k + 1, cols]
        xc_buf[:, cols] = part
        yield 256, (xc_buf,)

    log_sig = LRU_C * jax.nn.log_sigmoid(lam_ref[...])

    def gate_matmul(hh):
        park = slice((hh % 2) * 2 * hd, (hh % 2 + 1) * 2 * hd)
        mm_buf[:, park] = jnp.dot(xc_buf[:, hh * hd:(hh + 1) * hd].astype(BF16), wax_ref[hh],
                                  preferred_element_type=F32) + bax_ref[hh]

    def gate_math(hh, sub, rows):
        base = (hh % 2) * 2 * hd
        cols = slice(hh * hd + sub.start, hh * hd + sub.stop)
        gate_r = _sigmoid_of_twice(mm_buf[rows, base + sub.start:base + sub.stop])
        gate_i = _sigmoid_of_twice(mm_buf[rows, base + hd + sub.start:base + hd + sub.stop])
        log_a = gate_r * log_sig[:, cols]
        a_buf[rows, cols] = jnp.exp(log_a)
        t = jnp.tanh(log_a)
        q = -2.0 * t
        mult = jnp.where(q > 0.0, q * lax.rsqrt(q * (1.0 - t)), 0.0)
        u_buf[rows, cols] = mult * (gate_i * xc_buf[rows, cols])

    for first in range(0, heads, 2):
        for hh in range(first, min(first + 2, heads)):
            gate_matmul(hh)
            yield 0, (mm_buf,)
        for hh in range(first, min(first + 2, heads)):
            for sub in _splits(hd, hd // LANES):
                for rows in _splits(ts, 2):
                    gate_math(hh, sub, rows)
                    yield 240, (a_buf, u_buf)

    for cols in _splits(width, 2):
        n = cols.stop - cols.start
        row = lax.broadcasted_iota(jnp.int32, (SUBLANES, n), 0)
        seg_u = jnp.zeros((SUBLANES, n), F32)
        seg_a = jnp.ones((SUBLANES, n), F32)
        for j in range(SEG_LEN):
            blk = slice(j * SUBLANES, (j + 1) * SUBLANES)
            a = a_buf[blk, cols]
            seg_u = a * seg_u + u_buf[blk, cols]
            seg_a = a * seg_a
            u_buf[blk, cols] = seg_u
            a_buf[blk, cols] = seg_a
        k = 1
        while k < SUBLANES:
            keep = row >= k
            a_prev = jnp.where(keep, pltpu.roll(seg_a, k, 0), 1.0)
            u_prev = jnp.where(keep, pltpu.roll(seg_u, k, 0), 0.0)
            seg_u = seg_u + seg_a * u_prev
            seg_a = seg_a * a_prev
            k *= 2
        state_in = h_carry[:, cols]
        seg_end = seg_a * state_in + seg_u
        h_carry[:, cols] = seg_end[SUBLANES - 1:SUBLANES, :]
        start_buf[:, cols] = jnp.where(row == 0, state_in, pltpu.roll(seg_end, 1, 0))
        yield 200, (a_buf, u_buf)

    for cols in _splits(width, 2):
        mm_buf[:, cols] = (jnp.dot(hp_buf[...], wy_ref[:, cols], preferred_element_type=F32)
                           + by_ref[:, cols])
        yield 0, (mm_buf,)
    for cols in _splits(width, width // LANES):
        n = cols.stop - cols.start
        hs = (u_buf[:, cols].reshape(SEG_LEN, SUBLANES, n)
              + a_buf[:, cols].reshape(SEG_LEN, SUBLANES, n) * start_buf[:, cols][None]
              ).reshape(ts, n)
        hp_buf[:, cols] = (hs * _gelu_tanh(mm_buf[:, cols])).astype(BF16)
        yield 180, (hp_buf,)

    for cols in _splits(d, 2):
        y = gate[:, cols] * (jnp.dot(hp_buf[...], wout_ref[:, cols],
                                     preferred_element_type=F32) + bout_ref[:, cols])
        _store_slab(y, slab, cols)
        yield 0, (slab,)


def _pool_reset(scratch):
    (hbuf,) = scratch
    ts = hbuf.shape[0] - POOL_HALO * SUBLANES
    hbuf[ts:, :] = jnp.zeros((POOL_HALO * SUBLANES, hbuf.shape[1]), F32)


def _pool_phases(gate, slab, weights, scratch, tile_start):
    pw_ref, ps_ref = weights
    (hbuf,) = scratch
    halo = POOL_HALO * SUBLANES
    ts = hbuf.shape[0] - halo
    d = hbuf.shape[1]
    groups, gd, _ = pw_ref.shape
    row = lax.broadcasted_iota(jnp.int32, (SUBLANES, LANES), 0)

    order = list(range(SEG_LEN - POOL_HALO, SEG_LEN)) + list(range(SEG_LEN - POOL_HALO))
    for js in _splits(SEG_LEN, 2):
        for j in order[js]:
            for c in range(d // LANES):
                lanes = slice(c * LANES, (c + 1) * LANES)
                dst = slice(halo + j * SUBLANES, halo + (j + 1) * SUBLANES)
                new = _permuted_block(slab, j, c)
                if j >= SEG_LEN - POOL_HALO:
                    p = j - (SEG_LEN - POOL_HALO)
                    hbuf[p * SUBLANES:(p + 1) * SUBLANES, lanes] = _shift_segments(
                        hbuf[dst, lanes], new, row)
                hbuf[dst, lanes] = new
        yield 100, (hbuf,)

    m = lax.broadcasted_iota(jnp.int32, (ts, 1), 0)
    pos = tile_start + (m % SUBLANES) * SEG_LEN + m // SUBLANES + 1
    for gi, win in enumerate(POOL_WINDOWS):
        cols = slice(gi * gd, (gi + 1) * gd)
        total = hbuf[(POOL_HALO - (win - 1)) * SUBLANES:, cols]
        step = 1
        while step < win:
            total = total[step * SUBLANES:, :] + total[:-step * SUBLANES, :]
            step *= 2
        inv_count = 1.0 / jnp.minimum(pos, win).astype(F32)
        pooled = total * inv_count - hbuf[halo:, cols]
        mixed = jnp.dot(pooled.astype(BF16), pw_ref[gi], preferred_element_type=F32)
        _store_slab(gate[:, cols] * (mixed * ps_ref[:, cols]), slab, cols)
        yield 250, (slab,)


def _lru_costs(heads, hd, width):
    per_pair = [0, 0] + [240] * (2 * 2 * (hd // LANES))
    return ([256] + [100] * 4 + [100] * 2 + [0] * 2 + [256] * 4 + per_pair * (heads // 2)
            + [200] * 2 + [0] * 2 + [180] * (width // LANES) + [0] * 2)


def _pool_costs(groups):
    return [256] + [100] * 4 + [100] * 2 + [250] * groups


HAND_OVER_COSTS = [130, 130, 190, 190, 190, 190]


def _layer_kernel(*refs, kind, final, tiles_per_seq, n_mixer_weights, n_mixer_scratch):
    x_ref, modm_ref, modf_ref, gm_ref, gf_ref = refs[:5]
    mixer_weights = refs[5:5 + n_mixer_weights]
    rest = refs[5 + n_mixer_weights:]
    if final:
        w1_ref, w2_ref, fg_ref, o_ref = rest[:4]
        rest = rest[4:]
    else:
        w1_ref, w2_ref, o_ref = rest[:3]
        rest = rest[3:]
    slab, x1_buf, h1_buf, h_loc = rest[:4]
    mixer_scratch = rest[4:4 + n_mixer_scratch]
    ts, d = x_ref.shape
    g = pl.program_id(0)

    @pl.when(g == 0)
    def _():
        x1_buf[...] = jnp.zeros_like(x1_buf)
        h1_buf[...] = jnp.zeros_like(h1_buf)

    @pl.when(g % tiles_per_seq == 0)
    def _():
        (_lru_reset if kind == "lru" else _pool_reset)(mixer_scratch)

    gate = modm_ref[2:3, :]
    if kind == "lru":
        heads, hd, _ = mixer_weights[6].shape
        assert heads % 2 == 0 and 4 * hd <= d
        mixer = _lru_phases(gate, slab, mixer_weights, mixer_scratch)
        costs = _lru_costs(heads, hd, d)
    else:
        mixer = _pool_phases(gate, slab, mixer_weights, mixer_scratch,
                             (g % tiles_per_seq) * ts)
        costs = _pool_costs(mixer_weights[0].shape[0])
    costs = costs + HAND_OVER_COSTS

    def hand_over(qs):
        for q in qs:
            for r in range(SUBLANES):
                src = q * SUBLANES * SUBLANES + r
                dst = slice(r * SEG_LEN + q * SUBLANES, r * SEG_LEN + (q + 1) * SUBLANES)
                for c in range(d // LANES):
                    lanes = slice(c * LANES, (c + 1) * LANES)
                    x1_buf[dst, lanes] = (
                        x_ref[dst, lanes]
                        + slab[c, pl.ds(src, SUBLANES, stride=SUBLANES), :])

    def all_phases():
        yield from _norm_to_slab(x_ref, gm_ref[...], modm_ref[0:1, :], modm_ref[1:2, :], slab)
        yield from mixer
        for qs in _splits(SEG_LEN // SUBLANES, 2):
            hand_over(range(qs.start, qs.stop))
            yield 130, (x1_buf,)
        gs = gf_ref[...] * (1.0 + modm_ref[4:5, :])
        for rows in _splits(ts, 4):
            x1 = x1_buf[rows, :]
            h1_buf[rows, :] = ((x1 * _rms_scale(x1)) * gs + modm_ref[3:4, :]).astype(BF16)
            yield 190, (h1_buf,)

    phases = all_phases()
    total_cost = float(sum(costs) - sum(HAND_OVER_COSTS))
    spent = 0.0
    n_emitted = 0
    written = []

    def fill(fraction):
        nonlocal spent, n_emitted
        while n_emitted < len(costs) and spent < fraction * total_cost:
            cost, refs = next(phases)
            assert cost == costs[n_emitted], (n_emitted, cost, costs[n_emitted])
            spent += cost
            n_emitted += 1
            written.extend(r for r in refs if all(r is not w for w in written))

    def anchor(target):
        pair = 2 * SUBLANES
        rows = pl.ds(pl.multiple_of(lax.shift_right_logical(g, 30) * pair, pair), pair)
        bits = None
        for ref in written:
            tok = ref[0, rows, :] if len(ref.shape) == 3 else ref[rows, 0:LANES]
            b = pltpu.bitcast(tok, jnp.uint32)
            if b.shape[0] == pair:
                b = b[:SUBLANES] | b[SUBLANES:]
            bits = b if bits is None else bits | b
        del written[:]
        if bits is not None:
            zero = pltpu.bitcast(lax.shift_right_logical(bits, jnp.uint32(32)), BF16)
            target[0:pair, 0:LANES] = target[0:pair, 0:LANES] + zero

    d_ff = w1_ref.shape[1]
    n_chunks = d_ff // FF_CHUNK

    def up(k):
        h_in = h1_buf[...] if k == 0 else h_loc[...]
        cols = slice(k * FF_CHUNK, (k + 1) * FF_CHUNK)
        u = jnp.maximum(jnp.dot(h_in, w1_ref[:, cols], preferred_element_type=F32), 0.0)
        return (u * u).astype(BF16)

    acc = None
    for k in range(n_chunks):
        if k > 0:
            anchor(h_loc)
        u = up(k)
        if k == 0:
            o_ref[...] = x1_buf[...]
            h_loc[...] = h1_buf[...]
        fill((2 * k + 1) / (2 * n_chunks - 1))
        p = jnp.dot(u, w2_ref[k * FF_CHUNK:(k + 1) * FF_CHUNK, :], preferred_element_type=F32)
        acc = p if acc is None else acc + p
        fill((2 * k + 2) / (2 * n_chunks - 1) if k + 1 < n_chunks else 2.0)
    assert n_emitted == len(costs) and next(phases, "done") == "done"
    out = o_ref[...] + modf_ref[5:6, :] * acc
    if final:
        out = (out * _rms_scale(out)) * fg_ref[...]
    o_ref[...] = out


def _layer(x, mod, layer, kind, g_mix, g_ffn, mixer_ops, j, w1, w2, final_g=None):
    n_tiles, ts, d = x.shape
    bsz = mod.shape[1]
    tiles_per_seq = n_tiles // bsz
    final = final_g is not None
    last = n_tiles - 1

    def resident(op, idx):
        nd = op.ndim - 1
        return pl.BlockSpec((None,) + op.shape[1:], lambda g: (idx,) + (0,) * nd,
                            pipeline_mode=pl.Buffered(1))

    def mix_tile(g):
        return jnp.minimum(g, last)

    def mlp_tile(g):
        return jnp.maximum(g - 1, 0)

    operands = [x, mod, mod, g_mix, g_ffn, *mixer_ops, w1, w2]
    in_specs = [
        pl.BlockSpec((None, ts, d), lambda g: (mix_tile(g), 0, 0)),
        pl.BlockSpec((None, None, N_MOD, d),
                     lambda g: (layer, mix_tile(g) // tiles_per_seq, 0, 0)),
        pl.BlockSpec((None, None, N_MOD, d),
                     lambda g: (layer, mlp_tile(g) // tiles_per_seq, 0, 0)),
        resident(g_mix, layer), resident(g_ffn, layer),
        *[resident(op, j) for op in mixer_ops],
        resident(w1, layer), resident(w2, layer),
    ]
    if final:
        operands.append(final_g.reshape(1, 1, d))
        in_specs.append(resident(operands[-1], 0))

    scratch = [
        pltpu.VMEM((d // LANES, SUBLANES * SEG_PITCH, LANES), F32),
        pltpu.VMEM((ts, d), F32),
        pltpu.VMEM((ts, d), BF16),
        pltpu.VMEM((ts, d), BF16),
    ]
    if kind == "lru":
        width = mixer_ops[0].shape[2]
        assert width == d
        mixer_scratch = [
            pltpu.VMEM((ts, d), BF16),
            pltpu.VMEM((ts + CONV_HALO * SUBLANES, width), F32),
            pltpu.VMEM((ts, width), F32),
            pltpu.VMEM((ts, width), F32),
            pltpu.VMEM((ts, width), F32),
            pltpu.VMEM((ts, width), F32),
            pltpu.VMEM((SUBLANES, width), F32),
            pltpu.VMEM((1, width), F32),
        ]
    else:
        mixer_scratch = [pltpu.VMEM((ts + POOL_HALO * SUBLANES, d), F32)]

    return pl.pallas_call(
        functools.partial(_layer_kernel, kind=kind, final=final, tiles_per_seq=tiles_per_seq,
                          n_mixer_weights=len(mixer_ops), n_mixer_scratch=len(mixer_scratch)),
        grid=(n_tiles + 1,),
        in_specs=in_specs,
        out_specs=pl.BlockSpec((None, ts, d), lambda g: (mlp_tile(g), 0, 0)),
        out_shape=jax.ShapeDtypeStruct(x.shape, F32),
        scratch_shapes=scratch + mixer_scratch,
        compiler_params=pltpu.CompilerParams(
            dimension_semantics=("arbitrary",), vmem_limit_bytes=VMEM_LIMIT),
        name=kind + ("_layer_final" if final else "_layer"),
    )(*operands)


def _rows(v):
    return v.reshape(v.shape[0], 1, v.shape[1])


def kernel(x, c, w_mod, b_mod, norm_mix_g, norm_ffn_g, lru_w_y, lru_b_y, lru_w_in, lru_b_in,
           lru_conv_w, lru_conv_b, lru_w_a, lru_b_a, lru_w_x, lru_b_x, lru_lambda, lru_w_out,
           lru_b_out, pool_w, pool_scale, ffn_w1, ffn_w2, final_norm_g):
    depth = w_mod.shape[0]
    bsz, seq, d = x.shape
    assert seq % SEQ_TILE == 0 and ffn_w1.shape[2] % FF_CHUNK == 0
    assert w_mod.shape[2] % MOD_COLS == 0 and d % (4 * LANES) == 0
    mod = _modulation(c, w_mod, b_mod)

    n_a, heads, hd, _ = lru_w_a.shape
    w_ax = (0.5 * jnp.concatenate([lru_w_a, lru_w_x], axis=-1)).astype(BF16)
    b_ax = (0.5 * jnp.concatenate([lru_b_a, lru_b_x], axis=-1)).reshape(n_a, heads, 1, 2 * hd)
    lru_ops = (lru_w_y.astype(BF16), _rows(lru_b_y), lru_w_in.astype(BF16), _rows(lru_b_in),
               lru_conv_w, _rows(lru_conv_b), w_ax, b_ax, _rows(lru_lambda),
               lru_w_out.astype(BF16), _rows(lru_b_out))
    pool_ops = (pool_w.astype(BF16), _rows(pool_scale))
    g_mix, g_ffn = _rows(norm_mix_g), _rows(norm_ffn_g)
    w1, w2 = ffn_w1.astype(BF16), ffn_w2.astype(BF16)

    x = x.reshape(bsz * seq // SEQ_TILE, SEQ_TILE, d)
    for i in range(depth):
        j = i // N_MIXERS
        kind, ops = ("lru", lru_ops) if i % N_MIXERS == 0 else ("pool", pool_ops)
        x = _layer(x, mod, i, kind, g_mix, g_ffn, ops, j, w1, w2,
                   final_norm_g if i == depth - 1 else None)
    return x.reshape(bsz, seq, d)
```

```python
import functools
import math

import jax
import jax.numpy as jnp
from jax import lax
from jax.experimental import pallas as pl
from jax.experimental.pallas import tpu as pltpu

F32 = jnp.float32
BF16 = jnp.bfloat16

N_MIXERS = 2
N_MOD = 6
CONV_WIDTH = 4
LRU_C = 8.0
POOL_WINDOWS = (2, 4, 8, 16)
EPS = 1e-6

SUBLANES = 8
LANES = 128
SEQ_TILE = 512
SEG_LEN = SEQ_TILE // SUBLANES
SEG_PITCH = SEG_LEN + SUBLANES
FF_CHUNK = 1024
MOD_COLS = 1536
CONV_HALO = CONV_WIDTH - 1
POOL_HALO = max(POOL_WINDOWS)
VMEM_LIMIT = 58 * 1024 * 1024

GELU_C1 = math.sqrt(2.0 / math.pi)
GELU_C2 = GELU_C1 * 0.044715


def _rms_scale(x):
    return lax.rsqrt(jnp.mean(x * x, axis=-1, keepdims=True) + EPS)


def _splits(n, parts):
    step = n // parts
    return tuple(slice(i * step, (i + 1) * step) for i in range(parts))


def _mod_kernel(c_ref, w_ref, b_ref, o_ref):
    c = c_ref[...]
    cond = c * jax.nn.sigmoid(c)
    o_ref[...] = jnp.dot(cond.astype(BF16), w_ref[...].astype(BF16),
                         preferred_element_type=F32) + b_ref[...]


def _modulation(c, w_mod, b_mod):
    depth, d, e = w_mod.shape
    bsz = c.shape[0]
    rows = -(-bsz // SUBLANES) * SUBLANES
    c_pad = jnp.zeros((rows, d), F32).at[:bsz].set(c)
    out = pl.pallas_call(
        _mod_kernel,
        grid=(depth, e // MOD_COLS),
        in_specs=[
            pl.BlockSpec((rows, d), lambda i, j: (0, 0)),
            pl.BlockSpec((None, d, MOD_COLS), lambda i, j: (i, 0, j)),
            pl.BlockSpec((None, 1, MOD_COLS), lambda i, j: (i, 0, j)),
        ],
        out_specs=pl.BlockSpec((None, rows, MOD_COLS), lambda i, j: (i, 0, j)),
        out_shape=jax.ShapeDtypeStruct((depth, rows, e), F32),
        compiler_params=pltpu.CompilerParams(
            dimension_semantics=("arbitrary", "arbitrary"), vmem_limit_bytes=VMEM_LIMIT),
        name="adaln_mod",
    )(c_pad, w_mod, b_mod.reshape(depth, 1, e))
    return out[:, :bsz].reshape(depth, bsz, N_MOD, d)


def _norm_to_slab(x_ref, g, shift, scale, slab):
    d = x_ref.shape[1]
    r_scale = _rms_scale(x_ref[...])
    gs = g * (1.0 + scale)
    yield 256, ()
    for cols in _splits(d, 4):
        h = (x_ref[:, cols] * r_scale) * gs[:, cols] + shift[:, cols]
        for c in range(cols.start // LANES, cols.stop // LANES):
            for r in range(SUBLANES):
                slab[c, r * SEG_PITCH:r * SEG_PITCH + SEG_LEN, :] = (
                    h[r * SEG_LEN:(r + 1) * SEG_LEN,
                      c * LANES - cols.start:(c + 1) * LANES - cols.start])
        yield 100, (slab,)


def _permuted_block(slab, j, c):
    return slab[c, pl.ds(j, SUBLANES, stride=SEG_PITCH), :]


def _shift_segments(prev_block, cur_block, row):
    return jnp.where(row == 0, pltpu.roll(prev_block, 1, 0), pltpu.roll(cur_block, 1, 0))


def _store_slab(y, slab, cols):
    for c in range(cols.start // LANES, cols.stop // LANES):
        slab[c, 0:y.shape[0], :] = y[:, c * LANES - cols.start:(c + 1) * LANES - cols.start]


def _sigmoid_of_twice(v):
    return 0.5 * jnp.tanh(v) + 0.5


def _gelu_tanh(v):
    return (0.5 * v) * (1.0 + jnp.tanh(v * (GELU_C1 + GELU_C2 * (v * v))))


def _lru_reset(scratch):
    hp_buf, xbuf, xc_buf, mm_buf, a_buf, u_buf, start_buf, h_carry = scratch
    ts, width = a_buf.shape
    xbuf[ts:ts + CONV_HALO * SUBLANES, :] = jnp.zeros((CONV_HALO * SUBLANES, width), F32)
    h_carry[...] = jnp.zeros_like(h_carry)


def _lru_phases(gate, slab, weights, scratch):
    wy_ref, by_ref, win_ref, bin_ref, cw_ref, cb_ref, wax_ref, bax_ref, lam_ref, wout_ref, \
        bout_ref = weights
    hp_buf, xbuf, xc_buf, mm_buf, a_buf, u_buf, start_buf, h_carry = scratch
    ts, width = a_buf.shape
    d = hp_buf.shape[1]
    heads, hd, _ = wax_ref.shape
    halo = CONV_HALO * SUBLANES
    pair = 2 * SUBLANES

    for js in _splits(SEG_LEN, 2):
        for j in range(js.start, js.stop, 2):
            for c in range(d // LANES):
                both = jnp.concatenate(
                    [_permuted_block(slab, j, c), _permuted_block(slab, j + 1, c)], axis=0)
                hp_buf[j * SUBLANES:j * SUBLANES + pair, c * LANES:(c + 1) * LANES] = (
                    both.astype(BF16))
        yield 100, (hp_buf,)

    for cols in _splits(width, 2):
        n = cols.stop - cols.start
        row = lax.broadcasted_iota(jnp.int32, (SUBLANES, n), 0)
        prev_tail = xbuf[ts:ts + halo, cols]
        xr = jnp.dot(hp_buf[...], win_ref[:, cols], preferred_element_type=F32) + bin_ref[:, cols]
        xbuf[halo:halo + ts, cols] = xr
        for p in range(CONV_HALO):
            blk = slice(p * SUBLANES, (p + 1) * SUBLANES)
            cur = xr[ts - halo + p * SUBLANES:ts - halo + (p + 1) * SUBLANES, :]
            xbuf[blk, cols] = _shift_segments(prev_tail[blk, :], cur, row)
        yield 0, (xbuf,)
    for cols in _splits(width, 4):
        part = cb_ref[:, cols]
        for k in range(CONV_WIDTH):
            part = part + xbuf[k * SUBLANES:k * SUBLANES + ts, cols] * cw_ref[k:k + 1, cols]
        xc_buf[:, cols] = part
        yield 256, (xc_buf,)

    log_sig = LRU_C * jax.nn.log_sigmoid(lam_ref[...])

    def gate_matmul(hh):
        park = slice((hh % 2) * 2 * hd, (hh % 2 + 1) * 2 * hd)
        mm_buf[:, park] = jnp.dot(xc_buf[:, hh * hd:(hh + 1) * hd].astype(BF16), wax_ref[hh],
                                  preferred_element_type=F32) + bax_ref[hh]

    def gate_math(hh, sub, rows):
        base = (hh % 2) * 2 * hd
        cols = slice(hh * hd + sub.start, hh * hd + sub.stop)
        gate_r = _sigmoid_of_twice(mm_buf[rows, base + sub.start:base + sub.stop])
        gate_i = _sigmoid_of_twice(mm_buf[rows, base + hd + sub.start:base + hd + sub.stop])
        log_a = gate_r * log_sig[:, cols]
        a_buf[rows, cols] = jnp.exp(log_a)
        t = jnp.tanh(log_a)
        q = -2.0 * t
        mult = jnp.where(q > 0.0, q * lax.rsqrt(q * (1.0 - t)), 0.0)
        u_buf[rows, cols] = mult * (gate_i * xc_buf[rows, cols])

    for first in range(0, heads, 2):
        for hh in range(first, min(first + 2, heads)):
            gate_matmul(hh)
            yield 0, (mm_buf,)
        for hh in range(first, min(first + 2, heads)):
            for sub in _splits(hd, hd // LANES):
                for rows in _splits(ts, 2):
                    gate_math(hh, sub, rows)
                    yield 240, (a_buf, u_buf)

    for cols in _splits(width, 2):
        n = cols.stop - cols.start
        row = lax.broadcasted_iota(jnp.int32, (SUBLANES, n), 0)
        seg_u = jnp.zeros((SUBLANES, n), F32)
        seg_a = jnp.ones((SUBLANES, n), F32)
        for j in range(SEG_LEN):
            blk = slice(j * SUBLANES, (j + 1) * SUBLANES)
            a = a_buf[blk, cols]
            seg_u = a * seg_u + u_buf[blk, cols]
            seg_a = a * seg_a
            u_buf[blk, cols] = seg_u
            a_buf[blk, cols] = seg_a
        k = 1
        while k < SUBLANES:
            keep = row >= k
            a_prev = jnp.where(keep, pltpu.roll(seg_a, k, 0), 1.0)
            u_prev = jnp.where(keep, pltpu.roll(seg_u, k, 0), 0.0)
            seg_u = seg_u + seg_a * u_prev
            seg_a = seg_a * a_prev
            k *= 2
        state_in = h_carry[:, cols]
        seg_end = seg_a * state_in + seg_u
        h_carry[:, cols] = seg_end[SUBLANES - 1:SUBLANES, :]
        start_buf[:, cols] = jnp.where(row == 0, state_in, pltpu.roll(seg_end, 1, 0))
        yield 200, (a_buf, u_buf)

    for cols in _splits(width, 2):
        mm_buf[:, cols] = (jnp.dot(hp_buf[...], wy_ref[:, cols], preferred_element_type=F32)
                           + by_ref[:, cols])
        yield 0, (mm_buf,)
    for cols in _splits(width, width // LANES):
        n = cols.stop - cols.start
        hs = (u_buf[:, cols].reshape(SEG_LEN, SUBLANES, n)
              + a_buf[:, cols].reshape(SEG_LEN, SUBLANES, n) * start_buf[:, cols][None]
              ).reshape(ts, n)
        hp_buf[:, cols] = (hs * _gelu_tanh(mm_buf[:, cols])).astype(BF16)
        yield 180, (hp_buf,)

    for cols in _splits(d, 2):
        y = gate[:, cols] * (jnp.dot(hp_buf[...], wout_ref[:, cols],
                                     preferred_element_type=F32) + bout_ref[:, cols])
        _store_slab(y, slab, cols)
        yield 0, (slab,)


def _pool_reset(scratch):
    (hbuf,) = scratch
    ts = hbuf.shape[0] - POOL_HALO * SUBLANES
    hbuf[ts:, :] = jnp.zeros((POOL_HALO * SUBLANES, hbuf.shape[1]), F32)


def _pool_phases(gate, slab, weights, scratch, tile_start):
    pw_ref, ps_ref = weights
    (hbuf,) = scratch
    halo = POOL_HALO * SUBLANES
    ts = hbuf.shape[0] - halo
    d = hbuf.shape[1]
    groups, gd, _ = pw_ref.shape
    row = lax.broadcasted_iota(jnp.int32, (SUBLANES, LANES), 0)

    order = list(range(SEG_LEN - POOL_HALO, SEG_LEN)) + list(range(SEG_LEN - POOL_HALO))
    for js in _splits(SEG_LEN, 2):
        for j in order[js]:
            for c in range(d // LANES):
                lanes = slice(c * LANES, (c + 1) * LANES)
                dst = slice(halo + j * SUBLANES, halo + (j + 1) * SUBLANES)
                new = _permuted_block(slab, j, c)
                if j >= SEG_LEN - POOL_HALO:
                    p = j - (SEG_LEN - POOL_HALO)
                    hbuf[p * SUBLANES:(p + 1) * SUBLANES, lanes] = _shift_segments(
                        hbuf[dst, lanes], new, row)
                hbuf[dst, lanes] = new
        yield 100, (hbuf,)

    m = lax.broadcasted_iota(jnp.int32, (ts, 1), 0)
    pos = tile_start + (m % SUBLANES) * SEG_LEN + m // SUBLANES + 1
    for gi, win in enumerate(POOL_WINDOWS):
        cols = slice(gi * gd, (gi + 1) * gd)
        total = hbuf[(POOL_HALO - (win - 1)) * SUBLANES:, cols]
        step = 1
        while step < win:
            total = total[step * SUBLANES:, :] + total[:-step * SUBLANES, :]
            step *= 2
        inv_count = 1.0 / jnp.minimum(pos, win).astype(F32)
        pooled = total * inv_count - hbuf[halo:, cols]
        mixed = jnp.dot(pooled.astype(BF16), pw_ref[gi], preferred_element_type=F32)
        _store_slab(gate[:, cols] * (mixed * ps_ref[:, cols]), slab, cols)
        yield 250, (slab,)


def _lru_costs(heads, hd, width):
    per_pair = [0, 0] + [240] * (2 * 2 * (hd // LANES))
    return ([256] + [100] * 4 + [100] * 2 + [0] * 2 + [256] * 4 + per_pair * (heads // 2)
            + [200] * 2 + [0] * 2 + [180] * (width // LANES) + [0] * 2)


def _pool_costs(groups):
    return [256] + [100] * 4 + [100] * 2 + [250] * groups


HAND_OVER_COSTS = [130, 130, 190, 190, 190, 190]


def _layer_kernel(*refs, kind, final, tiles_per_seq, n_mixer_weights, n_mixer_scratch):
    x_ref, modm_ref, modf_ref, gm_ref, gf_ref = refs[:5]
    mixer_weights = refs[5:5 + n_mixer_weights]
    rest = refs[5 + n_mixer_weights:]
    if final:
        w1_ref, w2_ref, fg_ref, o_ref = rest[:4]
        rest = rest[4:]
    else:
        w1_ref, w2_ref, o_ref = rest[:3]
        rest = rest[3:]
    slab, x1_buf, h1_buf, h_loc = rest[:4]
    mixer_scratch = rest[4:4 + n_mixer_scratch]
    ts, d = x_ref.shape
    g = pl.program_id(0)

    def run_if(cond, body, trips=1):
        lax.fori_loop(0, jnp.where(cond, trips, 0), lambda i, c: (body(i), c)[1], 0)

    def clear_rows(i):
        pair = 2 * SUBLANES
        rows = pl.ds(pl.multiple_of(i * pair, pair), pair)
        x1_buf[rows, :] = jnp.zeros((pair, d), F32)
        h1_buf[rows, :] = jnp.zeros((pair, d), BF16)

    run_if(g == 0, clear_rows, trips=ts // (2 * SUBLANES))
    run_if(g % tiles_per_seq == 0,
           lambda i: (_lru_reset if kind == "lru" else _pool_reset)(mixer_scratch))

    gate = modm_ref[2:3, :]
    if kind == "lru":
        heads, hd, _ = mixer_weights[6].shape
        assert heads % 2 == 0 and 4 * hd <= d
        mixer = _lru_phases(gate, slab, mixer_weights, mixer_scratch)
        costs = _lru_costs(heads, hd, d)
    else:
        mixer = _pool_phases(gate, slab, mixer_weights, mixer_scratch,
                             (g % tiles_per_seq) * ts)
        costs = _pool_costs(mixer_weights[0].shape[0])
    costs = costs + HAND_OVER_COSTS

    def hand_over(qs):
        for q in qs:
            for r in range(SUBLANES):
                src = q * SUBLANES * SUBLANES + r
                dst = slice(r * SEG_LEN + q * SUBLANES, r * SEG_LEN + (q + 1) * SUBLANES)
                for c in range(d // LANES):
                    lanes = slice(c * LANES, (c + 1) * LANES)
                    x1_buf[dst, lanes] = (
                        x_ref[dst, lanes]
                        + slab[c, pl.ds(src, SUBLANES, stride=SUBLANES), :])

    def all_phases():
        yield from _norm_to_slab(x_ref, gm_ref[...], modm_ref[0:1, :], modm_ref[1:2, :], slab)
        yield from mixer
        for qs in _splits(SEG_LEN // SUBLANES, 2):
            hand_over(range(qs.start, qs.stop))
            yield 130, (x1_buf,)
        gs = gf_ref[...] * (1.0 + modm_ref[4:5, :])
        for rows in _splits(ts, 4):
            x1 = x1_buf[rows, :]
            h1_buf[rows, :] = ((x1 * _rms_scale(x1)) * gs + modm_ref[3:4, :]).astype(BF16)
            yield 190, (h1_buf,)

    phases = all_phases()
    total_cost = float(sum(costs) - sum(HAND_OVER_COSTS))
    spent = 0.0
    n_emitted = 0
    written = []

    def fill(fraction):
        nonlocal spent, n_emitted
        while n_emitted < len(costs) and spent < fraction * total_cost:
            cost, refs = next(phases)
            assert cost == costs[n_emitted], (n_emitted, cost, costs[n_emitted])
            spent += cost
            n_emitted += 1
            written.extend(r for r in refs if all(r is not w for w in written))

    def anchor(target):
        pair = 2 * SUBLANES
        rows = pl.ds(pl.multiple_of(lax.shift_right_logical(g, 30) * pair, pair), pair)
        bits = None
        for ref in written:
            tok = ref[0, rows, :] if len(ref.shape) == 3 else ref[rows, 0:LANES]
            b = pltpu.bitcast(tok, jnp.uint32)
            if b.shape[0] == pair:
                b = b[:SUBLANES] | b[SUBLANES:]
            bits = b if bits is None else bits | b
        del written[:]
        if bits is not None:
            zero = pltpu.bitcast(lax.shift_right_logical(bits, jnp.uint32(32)), BF16)
            target[0:pair, 0:LANES] = target[0:pair, 0:LANES] + zero

    d_ff = w1_ref.shape[1]
    n_chunks = d_ff // FF_CHUNK
    n_gaps = 2 * n_chunks - 1

    def up(k):
        h_in = h1_buf[...] if k == 0 else h_loc[...]
        cols = slice(k * FF_CHUNK, (k + 1) * FF_CHUNK)
        u = jnp.maximum(jnp.dot(h_in, w1_ref[:, cols], preferred_element_type=F32), 0.0)
        return (u * u).astype(BF16)

    acc = None
    for k in range(n_chunks):
        if k > 0:
            anchor(h_loc)
        u = up(k)
        if k == 0:
            o_ref[...] = x1_buf[...]
            h_loc[...] = h1_buf[...]
        fill((2 * k + 1) / n_gaps)
        p = jnp.dot(u, w2_ref[k * FF_CHUNK:(k + 1) * FF_CHUNK, :], preferred_element_type=F32)
        acc = p if acc is None else acc + p
        fill((2 * k + 2) / n_gaps if k + 1 < n_chunks else 2.0)
    assert n_emitted == len(costs) and next(phases, "done") == "done"
    out = o_ref[...] + modf_ref[5:6, :] * acc
    if final:
        out = (out * _rms_scale(out)) * fg_ref[...]
    o_ref[...] = out


def _layer(x, mod, layer, kind, g_mix, g_ffn, mixer_ops, j, w1, w2, final_g=None):
    n_tiles, ts, d = x.shape
    bsz = mod.shape[1]
    tiles_per_seq = n_tiles // bsz
    final = final_g is not None
    last = n_tiles - 1

    def resident(op, idx):
        nd = op.ndim - 1
        return pl.BlockSpec((None,) + op.shape[1:], lambda g: (idx,) + (0,) * nd,
                            pipeline_mode=pl.Buffered(1))

    def mix_tile(g):
        return jnp.minimum(g, last)

    def mlp_tile(g):
        return jnp.maximum(g - 1, 0)

    operands = [x, mod, mod, g_mix, g_ffn, *mixer_ops, w1, w2]
    in_specs = [
        pl.BlockSpec((None, ts, d), lambda g: (mix_tile(g), 0, 0)),
        pl.BlockSpec((None, None, N_MOD, d),
                     lambda g: (layer, mix_tile(g) // tiles_per_seq, 0, 0)),
        pl.BlockSpec((None, None, N_MOD, d),
                     lambda g: (layer, mlp_tile(g) // tiles_per_seq, 0, 0)),
        resident(g_mix, layer), resident(g_ffn, layer),
        *[resident(op, j) for op in mixer_ops],
        resident(w1, layer), resident(w2, layer),
    ]
    if final:
        operands.append(final_g.reshape(1, 1, d))
        in_specs.append(resident(operands[-1], 0))

    scratch = [
        pltpu.VMEM((d // LANES, SUBLANES * SEG_PITCH, LANES), F32),
        pltpu.VMEM((ts, d), F32),
        pltpu.VMEM((ts, d), BF16),
        pltpu.VMEM((ts, d), BF16),
    ]
    if kind == "lru":
        width = mixer_ops[0].shape[2]
        assert width == d
        mixer_scratch = [
            pltpu.VMEM((ts, d), BF16),
            pltpu.VMEM((ts + CONV_HALO * SUBLANES, width), F32),
            pltpu.VMEM((ts, width), F32),
            pltpu.VMEM((ts, width), F32),
            pltpu.VMEM((ts, width), F32),
            pltpu.VMEM((ts, width), F32),
            pltpu.VMEM((SUBLANES, width), F32),
            pltpu.VMEM((1, width), F32),
        ]
    else:
        mixer_scratch = [pltpu.VMEM((ts + POOL_HALO * SUBLANES, d), F32)]

    return pl.pallas_call(
        functools.partial(_layer_kernel, kind=kind, final=final, tiles_per_seq=tiles_per_seq,
                          n_mixer_weights=len(mixer_ops), n_mixer_scratch=len(mixer_scratch)),
        grid=(n_tiles + 1,),
        in_specs=in_specs,
        out_specs=pl.BlockSpec((None, ts, d), lambda g: (mlp_tile(g), 0, 0)),
        out_shape=jax.ShapeDtypeStruct(x.shape, F32),
        scratch_shapes=scratch + mixer_scratch,
        compiler_params=pltpu.CompilerParams(
            dimension_semantics=("arbitrary",), vmem_limit_bytes=VMEM_LIMIT),
        name=kind + ("_layer_final" if final else "_layer"),
    )(*operands)


def _rows(v):
    return v.reshape(v.shape[0], 1, v.shape[1])


def kernel(x, c, w_mod, b_mod, norm_mix_g, norm_ffn_g, lru_w_y, lru_b_y, lru_w_in, lru_b_in,
           lru_conv_w, lru_conv_b, lru_w_a, lru_b_a, lru_w_x, lru_b_x, lru_lambda, lru_w_out,
           lru_b_out, pool_w, pool_scale, ffn_w1, ffn_w2, final_norm_g):
    depth = w_mod.shape[0]
    bsz, seq, d = x.shape
    assert seq % SEQ_TILE == 0 and ffn_w1.shape[2] % FF_CHUNK == 0
    assert w_mod.shape[2] % MOD_COLS == 0 and d % (4 * LANES) == 0
    mod = _modulation(c, w_mod, b_mod)

    n_a, heads, hd, _ = lru_w_a.shape
    w_ax = (0.5 * jnp.concatenate([lru_w_a, lru_w_x], axis=-1)).astype(BF16)
    b_ax = (0.5 * jnp.concatenate([lru_b_a, lru_b_x], axis=-1)).reshape(n_a, heads, 1, 2 * hd)
    lru_ops = (lru_w_y.astype(BF16), _rows(lru_b_y), lru_w_in.astype(BF16), _rows(lru_b_in),
               lru_conv_w, _rows(lru_conv_b), w_ax, b_ax, _rows(lru_lambda),
               lru_w_out.astype(BF16), _rows(lru_b_out))
    pool_ops = (pool_w.astype(BF16), _rows(pool_scale))
    g_mix, g_ffn = _rows(norm_mix_g), _rows(norm_ffn_g)
    w1, w2 = ffn_w1.astype(BF16), ffn_w2.astype(BF16)

    x = x.reshape(bsz * seq // SEQ_TILE, SEQ_TILE, d)
    for i in range(depth):
        j = i // N_MIXERS
        kind, ops = ("lru", lru_ops) if i % N_MIXERS == 0 else ("pool", pool_ops)
        x = _layer(x, mod, i, kind, g_mix, g_ffn, ops, j, w1, w2,
                   final_norm_g if i == depth - 1 else None)
    return x.reshape(bsz, seq, d)
```

```python
import functools
import math

import jax
import jax.numpy as jnp
from jax import lax
from jax.experimental import pallas as pl
from jax.experimental.pallas import tpu as pltpu

F32 = jnp.float32
BF16 = jnp.bfloat16

N_MIXERS = 2
N_MOD = 6
CONV_WIDTH = 4
LRU_C = 8.0
POOL_WINDOWS = (2, 4, 8, 16)
EPS = 1e-6

SUBLANES = 8
LANES = 128
SEQ_TILE = 512
SEG_LEN = SEQ_TILE // SUBLANES
SEG_PITCH = SEG_LEN + SUBLANES
FF_CHUNK = 1024
MOD_COLS = 1536
CONV_HALO = CONV_WIDTH - 1
POOL_HALO = max(POOL_WINDOWS)
VMEM_LIMIT = 58 * 1024 * 1024

GELU_C1 = math.sqrt(2.0 / math.pi)
GELU_C2 = GELU_C1 * 0.044715


def _rms_scale(x):
    return lax.rsqrt(jnp.mean(x * x, axis=-1, keepdims=True) + EPS)


def _splits(n, parts):
    step = n // parts
    return tuple(slice(i * step, (i + 1) * step) for i in range(parts))


def _mod_kernel(c_ref, w_ref, b_ref, o_ref):
    c = c_ref[...]
    cond = c * jax.nn.sigmoid(c)
    o_ref[...] = jnp.dot(cond.astype(BF16), w_ref[...].astype(BF16),
                         preferred_element_type=F32) + b_ref[...]


def _modulation(c, w_mod, b_mod):
    depth, d, e = w_mod.shape
    bsz = c.shape[0]
    rows = -(-bsz // SUBLANES) * SUBLANES
    c_pad = jnp.zeros((rows, d), F32).at[:bsz].set(c)
    out = pl.pallas_call(
        _mod_kernel,
        grid=(depth, e // MOD_COLS),
        in_specs=[
            pl.BlockSpec((rows, d), lambda i, j: (0, 0)),
            pl.BlockSpec((None, d, MOD_COLS), lambda i, j: (i, 0, j)),
            pl.BlockSpec((None, 1, MOD_COLS), lambda i, j: (i, 0, j)),
        ],
        out_specs=pl.BlockSpec((None, rows, MOD_COLS), lambda i, j: (i, 0, j)),
        out_shape=jax.ShapeDtypeStruct((depth, rows, e), F32),
        compiler_params=pltpu.CompilerParams(
            dimension_semantics=("arbitrary", "arbitrary"), vmem_limit_bytes=VMEM_LIMIT),
        name="adaln_mod",
    )(c_pad, w_mod, b_mod.reshape(depth, 1, e))
    return out[:, :bsz].reshape(depth, bsz, N_MOD, d)


def _norm_to_slab(x_ref, g, shift, scale, slab):
    d = x_ref.shape[1]
    r_scale = _rms_scale(x_ref[...])
    gs = g * (1.0 + scale)
    yield 256, ()
    for cols in _splits(d, 4):
        h = (x_ref[:, cols] * r_scale) * gs[:, cols] + shift[:, cols]
        for c in range(cols.start // LANES, cols.stop // LANES):
            for r in range(SUBLANES):
                slab[c, r * SEG_PITCH:r * SEG_PITCH + SEG_LEN, :] = (
                    h[r * SEG_LEN:(r + 1) * SEG_LEN,
                      c * LANES - cols.start:(c + 1) * LANES - cols.start])
        yield 100, (slab,)


def _permuted_block(slab, j, c):
    return slab[c, pl.ds(j, SUBLANES, stride=SEG_PITCH), :]


def _shift_segments(prev_block, cur_block, row):
    return jnp.where(row == 0, pltpu.roll(prev_block, 1, 0), pltpu.roll(cur_block, 1, 0))


def _store_slab(y, slab, cols):
    for c in range(cols.start // LANES, cols.stop // LANES):
        slab[c, 0:y.shape[0], :] = y[:, c * LANES - cols.start:(c + 1) * LANES - cols.start]


def _sigmoid_of_twice(v):
    return 0.5 * jnp.tanh(v) + 0.5


def _gelu_tanh(v):
    return (0.5 * v) * (1.0 + jnp.tanh(v * (GELU_C1 + GELU_C2 * (v * v))))


def _lru_reset(scratch):
    hp_buf, xbuf, xc_buf, mm_buf, a_buf, u_buf, start_buf, h_carry = scratch
    ts, width = a_buf.shape
    xbuf[ts:ts + CONV_HALO * SUBLANES, :] = jnp.zeros((CONV_HALO * SUBLANES, width), F32)
    h_carry[...] = jnp.zeros_like(h_carry)


def _lru_phases(gate, slab, weights, scratch):
    wy_ref, by_ref, win_ref, bin_ref, cw_ref, cb_ref, wax_ref, bax_ref, lam_ref, wout_ref, \
        bout_ref = weights
    hp_buf, xbuf, xc_buf, mm_buf, a_buf, u_buf, start_buf, h_carry = scratch
    ts, width = a_buf.shape
    d = hp_buf.shape[1]
    heads, hd, _ = wax_ref.shape
    halo = CONV_HALO * SUBLANES
    pair = 2 * SUBLANES

    for js in _splits(SEG_LEN, 2):
        for j in range(js.start, js.stop, 2):
            for c in range(d // LANES):
                both = jnp.concatenate(
                    [_permuted_block(slab, j, c), _permuted_block(slab, j + 1, c)], axis=0)
                hp_buf[j * SUBLANES:j * SUBLANES + pair, c * LANES:(c + 1) * LANES] = (
                    both.astype(BF16))
        yield 100, (hp_buf,)

    for cols in _splits(width, 2):
        n = cols.stop - cols.start
        row = lax.broadcasted_iota(jnp.int32, (SUBLANES, n), 0)
        prev_tail = xbuf[ts:ts + halo, cols]
        xr = jnp.dot(hp_buf[...], win_ref[:, cols], preferred_element_type=F32) + bin_ref[:, cols]
        xbuf[halo:halo + ts, cols] = xr
        for p in range(CONV_HALO):
            blk = slice(p * SUBLANES, (p + 1) * SUBLANES)
            cur = xr[ts - halo + p * SUBLANES:ts - halo + (p + 1) * SUBLANES, :]
            xbuf[blk, cols] = _shift_segments(prev_tail[blk, :], cur, row)
        yield 0, (xbuf,)
    for cols in _splits(width, 4):
        part = cb_ref[:, cols]
        for k in range(CONV_WIDTH):
            part = part + xbuf[k * SUBLANES:k * SUBLANES + ts, cols] * cw_ref[k:k + 1, cols]
        xc_buf[:, cols] = part
        yield 256, (xc_buf,)

    log_sig = LRU_C * jax.nn.log_sigmoid(lam_ref[...])

    def gate_matmul(hh):
        park = slice((hh % 2) * 2 * hd, (hh % 2 + 1) * 2 * hd)
        mm_buf[:, park] = jnp.dot(xc_buf[:, hh * hd:(hh + 1) * hd].astype(BF16), wax_ref[hh],
                                  preferred_element_type=F32) + bax_ref[hh]

    def gate_math(hh, sub, rows):
        base = (hh % 2) * 2 * hd
        cols = slice(hh * hd + sub.start, hh * hd + sub.stop)
        gate_r = _sigmoid_of_twice(mm_buf[rows, base + sub.start:base + sub.stop])
        gate_i = _sigmoid_of_twice(mm_buf[rows, base + hd + sub.start:base + hd + sub.stop])
        log_a = gate_r * log_sig[:, cols]
        a_buf[rows, cols] = jnp.exp(log_a)
        t = jnp.tanh(log_a)
        q = -2.0 * t
        mult = jnp.where(q > 0.0, q * lax.rsqrt(q * (1.0 - t)), 0.0)
        u_buf[rows, cols] = mult * (gate_i * xc_buf[rows, cols])

    for first in range(0, heads, 2):
        for hh in range(first, min(first + 2, heads)):
            gate_matmul(hh)
            yield 0, (mm_buf,)
        for hh in range(first, min(first + 2, heads)):
            for sub in _splits(hd, hd // LANES):
                for rows in _splits(ts, 2):
                    gate_math(hh, sub, rows)
                    yield 240, (a_buf, u_buf)

    for cols in _splits(width, 2):
        n = cols.stop - cols.start
        row = lax.broadcasted_iota(jnp.int32, (SUBLANES, n), 0)
        seg_u = jnp.zeros((SUBLANES, n), F32)
        seg_a = jnp.ones((SUBLANES, n), F32)
        for j in range(SEG_LEN):
            blk = slice(j * SUBLANES, (j + 1) * SUBLANES)
            a = a_buf[blk, cols]
            seg_u = a * seg_u + u_buf[blk, cols]
            seg_a = a * seg_a
            u_buf[blk, cols] = seg_u
            a_buf[blk, cols] = seg_a
        k = 1
        while k < SUBLANES:
            keep = row >= k
            a_prev = jnp.where(keep, pltpu.roll(seg_a, k, 0), 1.0)
            u_prev = jnp.where(keep, pltpu.roll(seg_u, k, 0), 0.0)
            seg_u = seg_u + seg_a * u_prev
            seg_a = seg_a * a_prev
            k *= 2
        state_in = h_carry[:, cols]
        seg_end = seg_a * state_in + seg_u
        h_carry[:, cols] = seg_end[SUBLANES - 1:SUBLANES, :]
        start_buf[:, cols] = jnp.where(row == 0, state_in, pltpu.roll(seg_end, 1, 0))
        yield 200, (a_buf, u_buf)

    for cols in _splits(width, 2):
        mm_buf[:, cols] = (jnp.dot(hp_buf[...], wy_ref[:, cols], preferred_element_type=F32)
                           + by_ref[:, cols])
        yield 0, (mm_buf,)
    for cols in _splits(width, width // LANES):
        n = cols.stop - cols.start
        hs = (u_buf[:, cols].reshape(SEG_LEN, SUBLANES, n)
              + a_buf[:, cols].reshape(SEG_LEN, SUBLANES, n) * start_buf[:, cols][None]
              ).reshape(ts, n)
        hp_buf[:, cols] = (hs * _gelu_tanh(mm_buf[:, cols])).astype(BF16)
        yield 180, (hp_buf,)

    for cols in _splits(d, 2):
        y = gate[:, cols] * (jnp.dot(hp_buf[...], wout_ref[:, cols],
                                     preferred_element_type=F32) + bout_ref[:, cols])
        _store_slab(y, slab, cols)
        yield 0, (slab,)


def _pool_reset(scratch):
    (hbuf,) = scratch
    ts = hbuf.shape[0] - POOL_HALO * SUBLANES
    hbuf[ts:, :] = jnp.zeros((POOL_HALO * SUBLANES, hbuf.shape[1]), F32)


def _pool_phases(gate, slab, weights, scratch, tile_start):
    pw_ref, ps_ref = weights
    (hbuf,) = scratch
    halo = POOL_HALO * SUBLANES
    ts = hbuf.shape[0] - halo
    d = hbuf.shape[1]
    groups, gd, _ = pw_ref.shape
    row = lax.broadcasted_iota(jnp.int32, (SUBLANES, LANES), 0)

    order = list(range(SEG_LEN - POOL_HALO, SEG_LEN)) + list(range(SEG_LEN - POOL_HALO))
    for js in _splits(SEG_LEN, 2):
        for j in order[js]:
            for c in range(d // LANES):
                lanes = slice(c * LANES, (c + 1) * LANES)
                dst = slice(halo + j * SUBLANES, halo + (j + 1) * SUBLANES)
                new = _permuted_block(slab, j, c)
                if j >= SEG_LEN - POOL_HALO:
                    p = j - (SEG_LEN - POOL_HALO)
                    hbuf[p * SUBLANES:(p + 1) * SUBLANES, lanes] = _shift_segments(
                        hbuf[dst, lanes], new, row)
                hbuf[dst, lanes] = new
        yield 100, (hbuf,)

    m = lax.broadcasted_iota(jnp.int32, (ts, 1), 0)
    pos = tile_start + (m % SUBLANES) * SEG_LEN + m // SUBLANES + 1
    for gi, win in enumerate(POOL_WINDOWS):
        cols = slice(gi * gd, (gi + 1) * gd)
        total = hbuf[(POOL_HALO - (win - 1)) * SUBLANES:, cols]
        step = 1
        while step < win:
            total = total[step * SUBLANES:, :] + total[:-step * SUBLANES, :]
            step *= 2
        inv_count = 1.0 / jnp.minimum(pos, win).astype(F32)
        pooled = total * inv_count - hbuf[halo:, cols]
        mixed = jnp.dot(pooled.astype(BF16), pw_ref[gi], preferred_element_type=F32)
        _store_slab(gate[:, cols] * (mixed * ps_ref[:, cols]), slab, cols)
        yield 250, (slab,)


def _lru_costs(heads, hd, width):
    per_pair = [0, 0] + [240] * (2 * 2 * (hd // LANES))
    return ([256] + [100] * 4 + [100] * 2 + [0] * 2 + [256] * 4 + per_pair * (heads // 2)
            + [200] * 2 + [0] * 2 + [180] * (width // LANES) + [0] * 2)


def _pool_costs(groups):
    return [256] + [100] * 4 + [100] * 2 + [250] * groups


HAND_OVER_COSTS = [130, 130, 190, 190, 190, 190]


def _layer_kernel(*refs, kind, final, tiles_per_seq, n_mixer_weights, n_mixer_scratch):
    x_ref, modm_ref, modf_ref, gm_ref, gf_ref = refs[:5]
    mixer_weights = refs[5:5 + n_mixer_weights]
    rest = refs[5 + n_mixer_weights:]
    if final:
        w1_ref, w2_ref, fg_ref, o_ref = rest[:4]
        rest = rest[4:]
    else:
        w1_ref, w2_ref, o_ref = rest[:3]
        rest = rest[3:]
    slab, x1_buf, h1_buf, h_loc, u_all = rest[:5]
    mixer_scratch = rest[5:5 + n_mixer_scratch]
    ts, d = x_ref.shape
    g = pl.program_id(0)

    def run_if(cond, body, trips=1):
        lax.fori_loop(0, jnp.where(cond, trips, 0), lambda i, c: (body(i), c)[1], 0)

    def clear_rows(i):
        pair = 2 * SUBLANES
        rows = pl.ds(pl.multiple_of(i * pair, pair), pair)
        x1_buf[rows, :] = jnp.zeros((pair, d), F32)
        h1_buf[rows, :] = jnp.zeros((pair, d), BF16)

    run_if(g == 0, clear_rows, trips=ts // (2 * SUBLANES))
    run_if(g % tiles_per_seq == 0,
           lambda i: (_lru_reset if kind == "lru" else _pool_reset)(mixer_scratch))

    gate = modm_ref[2:3, :]
    if kind == "lru":
        heads, hd, _ = mixer_weights[6].shape
        assert heads % 2 == 0 and 4 * hd <= d
        mixer = _lru_phases(gate, slab, mixer_weights, mixer_scratch)
        costs = _lru_costs(heads, hd, d)
    else:
        mixer = _pool_phases(gate, slab, mixer_weights, mixer_scratch,
                             (g % tiles_per_seq) * ts)
        costs = _pool_costs(mixer_weights[0].shape[0])
    costs = costs + HAND_OVER_COSTS

    def hand_over(qs):
        for q in qs:
            for r in range(SUBLANES):
                src = q * SUBLANES * SUBLANES + r
                dst = slice(r * SEG_LEN + q * SUBLANES, r * SEG_LEN + (q + 1) * SUBLANES)
                for c in range(d // LANES):
                    lanes = slice(c * LANES, (c + 1) * LANES)
                    x1_buf[dst, lanes] = (
                        x_ref[dst, lanes]
                        + slab[c, pl.ds(src, SUBLANES, stride=SUBLANES), :])

    def all_phases():
        yield from _norm_to_slab(x_ref, gm_ref[...], modm_ref[0:1, :], modm_ref[1:2, :], slab)
        yield from mixer
        for qs in _splits(SEG_LEN // SUBLANES, 2):
            hand_over(range(qs.start, qs.stop))
            yield 130, (x1_buf,)
        gs = gf_ref[...] * (1.0 + modm_ref[4:5, :])
        for rows in _splits(ts, 4):
            x1 = x1_buf[rows, :]
            h1_buf[rows, :] = ((x1 * _rms_scale(x1)) * gs + modm_ref[3:4, :]).astype(BF16)
            yield 190, (h1_buf,)

    phases = all_phases()
    total_cost = float(sum(costs) - sum(HAND_OVER_COSTS))
    spent = 0.0
    n_emitted = 0
    written = []

    def fill(fraction):
        nonlocal spent, n_emitted
        while n_emitted < len(costs) and spent < fraction * total_cost:
            cost, refs = next(phases)
            assert cost == costs[n_emitted], (n_emitted, cost, costs[n_emitted])
            spent += cost
            n_emitted += 1
            written.extend(r for r in refs if all(r is not w for w in written))

    def anchor(target):
        pair = 2 * SUBLANES
        rows = pl.ds(pl.multiple_of(lax.shift_right_logical(g, 30) * pair, pair), pair)
        bits = None
        for ref in written:
            tok = ref[0, rows, :] if len(ref.shape) == 3 else ref[rows, 0:LANES]
            b = pltpu.bitcast(tok, jnp.uint32)
            if b.shape[0] == pair:
                b = b[:SUBLANES] | b[SUBLANES:]
            bits = b if bits is None else bits | b
        del written[:]
        if bits is not None:
            zero = pltpu.bitcast(lax.shift_right_logical(bits, jnp.uint32(32)), BF16)
            target[0:pair, 0:LANES] = target[0:pair, 0:LANES] + zero

    d_ff = w1_ref.shape[1]
    n_chunks = d_ff // FF_CHUNK
    for k in range(n_chunks):
        if k > 0:
            anchor(h_loc)
        h_in = h1_buf[...] if k == 0 else h_loc[...]
        cols = slice(k * FF_CHUNK, (k + 1) * FF_CHUNK)
        u = jnp.maximum(jnp.dot(h_in, w1_ref[:, cols], preferred_element_type=F32), 0.0)
        u_all[:, cols] = (u * u).astype(BF16)
        if k == 0:
            o_ref[...] = x1_buf[...]
            h_loc[...] = h1_buf[...]
        fill((k + 1) / n_chunks)
    fill(2.0)
    assert n_emitted == len(costs) and next(phases, "done") == "done"
    out = o_ref[...] + modf_ref[5:6, :] * jnp.dot(u_all[...], w2_ref[...],
                                                  preferred_element_type=F32)
    if final:
        out = (out * _rms_scale(out)) * fg_ref[...]
    o_ref[...] = out


def _layer(x, mod, layer, kind, g_mix, g_ffn, mixer_ops, j, w1, w2, final_g=None):
    n_tiles, ts, d = x.shape
    bsz = mod.shape[1]
    tiles_per_seq = n_tiles // bsz
    final = final_g is not None
    last = n_tiles - 1

    def resident(op, idx):
        nd = op.ndim - 1
        return pl.BlockSpec((None,) + op.shape[1:], lambda g: (idx,) + (0,) * nd,
                            pipeline_mode=pl.Buffered(1))

    def mix_tile(g):
        return jnp.minimum(g, last)

    def mlp_tile(g):
        return jnp.maximum(g - 1, 0)

    operands = [x, mod, mod, g_mix, g_ffn, *mixer_ops, w1, w2]
    in_specs = [
        pl.BlockSpec((None, ts, d), lambda g: (mix_tile(g), 0, 0)),
        pl.BlockSpec((None, None, N_MOD, d),
                     lambda g: (layer, mix_tile(g) // tiles_per_seq, 0, 0)),
        pl.BlockSpec((None, None, N_MOD, d),
                     lambda g: (layer, mlp_tile(g) // tiles_per_seq, 0, 0)),
        resident(g_mix, layer), resident(g_ffn, layer),
        *[resident(op, j) for op in mixer_ops],
        resident(w1, layer), resident(w2, layer),
    ]
    if final:
        operands.append(final_g.reshape(1, 1, d))
        in_specs.append(resident(operands[-1], 0))

    scratch = [
        pltpu.VMEM((d // LANES, SUBLANES * SEG_PITCH, LANES), F32),
        pltpu.VMEM((ts, d), F32),
        pltpu.VMEM((ts, d), BF16),
        pltpu.VMEM((ts, d), BF16),
        pltpu.VMEM((ts, w1.shape[2]), BF16),
    ]
    if kind == "lru":
        width = mixer_ops[0].shape[2]
        assert width == d
        mixer_scratch = [
            pltpu.VMEM((ts, d), BF16),
            pltpu.VMEM((ts + CONV_HALO * SUBLANES, width), F32),
            pltpu.VMEM((ts, width), F32),
            pltpu.VMEM((ts, width), F32),
            pltpu.VMEM((ts, width), F32),
            pltpu.VMEM((ts, width), F32),
            pltpu.VMEM((SUBLANES, width), F32),
            pltpu.VMEM((1, width), F32),
        ]
    else:
        mixer_scratch = [pltpu.VMEM((ts + POOL_HALO * SUBLANES, d), F32)]

    return pl.pallas_call(
        functools.partial(_layer_kernel, kind=kind, final=final, tiles_per_seq=tiles_per_seq,
                          n_mixer_weights=len(mixer_ops), n_mixer_scratch=len(mixer_scratch)),
        grid=(n_tiles + 1,),
        in_specs=in_specs,
        out_specs=pl.BlockSpec((None, ts, d), lambda g: (mlp_tile(g), 0, 0)),
        out_shape=jax.ShapeDtypeStruct(x.shape, F32),
        scratch_shapes=scratch + mixer_scratch,
        compiler_params=pltpu.CompilerParams(
            dimension_semantics=("arbitrary",), vmem_limit_bytes=VMEM_LIMIT),
        name=kind + ("_layer_final" if final else "_layer"),
    )(*operands)


def _rows(v):
    return v.reshape(v.shape[0], 1, v.shape[1])


def kernel(x, c, w_mod, b_mod, norm_mix_g, norm_ffn_g, lru_w_y, lru_b_y, lru_w_in, lru_b_in,
           lru_conv_w, lru_conv_b, lru_w_a, lru_b_a, lru_w_x, lru_b_x, lru_lambda, lru_w_out,
           lru_b_out, pool_w, pool_scale, ffn_w1, ffn_w2, final_norm_g):
    depth = w_mod.shape[0]
    bsz, seq, d = x.shape
    assert seq % SEQ_TILE == 0 and ffn_w1.shape[2] % FF_CHUNK == 0
    assert w_mod.shape[2] % MOD_COLS == 0 and d % (4 * LANES) == 0
    mod = _modulation(c, w_mod, b_mod)

    n_a, heads, hd, _ = lru_w_a.shape
    w_ax = (0.5 * jnp.concatenate([lru_w_a, lru_w_x], axis=-1)).astype(BF16)
    b_ax = (0.5 * jnp.concatenate([lru_b_a, lru_b_x], axis=-1)).reshape(n_a, heads, 1, 2 * hd)
    lru_ops = (lru_w_y.astype(BF16), _rows(lru_b_y), lru_w_in.astype(BF16), _rows(lru_b_in),
               lru_conv_w, _rows(lru_conv_b), w_ax, b_ax, _rows(lru_lambda),
               lru_w_out.astype(BF16), _rows(lru_b_out))
    pool_ops = (pool_w.astype(BF16), _rows(pool_scale))
    g_mix, g_ffn = _rows(norm_mix_g), _rows(norm_ffn_g)
    w1, w2 = ffn_w1.astype(BF16), ffn_w2.astype(BF16)

    x = x.reshape(bsz * seq // SEQ_TILE, SEQ_TILE, d)
    for i in range(depth):
        j = i // N_MIXERS
        kind, ops = ("lru", lru_ops) if i % N_MIXERS == 0 else ("pool", pool_ops)
        x = _layer(x, mod, i, kind, g_mix, g_ffn, ops, j, w1, w2,
                   final_norm_g if i == depth - 1 else None)
    return x.reshape(bsz, seq, d)
```

```python
import functools
import math

import jax
import jax.numpy as jnp
from jax import lax
from jax.experimental import pallas as pl
from jax.experimental.pallas import tpu as pltpu

F32 = jnp.float32
BF16 = jnp.bfloat16

N_MIXERS = 2
N_MOD = 6
CONV_WIDTH = 4
LRU_C = 8.0
POOL_WINDOWS = (2, 4, 8, 16)
EPS = 1e-6

SUBLANES = 8
LANES = 128
SEQ_TILE = 512
SEG_LEN = SEQ_TILE // SUBLANES
SEG_PITCH = SEG_LEN + SUBLANES
FF_CHUNK = 1024
MOD_COLS = 1536
CONV_HALO = CONV_WIDTH - 1
POOL_HALO = max(POOL_WINDOWS)
VMEM_LIMIT = 58 * 1024 * 1024

GELU_C1 = math.sqrt(2.0 / math.pi)
GELU_C2 = GELU_C1 * 0.044715


def _rms_scale(x):
    return lax.rsqrt(jnp.mean(x * x, axis=-1, keepdims=True) + EPS)


def _splits(n, parts):
    step = n // parts
    return tuple(slice(i * step, (i + 1) * step) for i in range(parts))


class _Rows:
    def __init__(self, ref, first, count=1):
        self.ref, self.first, self.count = ref, first, count

    def __getitem__(self, idx):
        if idx is Ellipsis:
            return self.ref[self.first:self.first + self.count, :]
        rows, cols = idx
        start = self.first + (rows.start or 0)
        stop = self.first + (self.count if rows.stop is None else rows.stop)
        return self.ref[start:stop, cols]


def _mod_kernel(c_ref, w_ref, b_ref, o_ref):
    c = c_ref[...]
    cond = c * jax.nn.sigmoid(c)
    o_ref[...] = jnp.dot(cond.astype(BF16), w_ref[...].astype(BF16),
                         preferred_element_type=F32) + b_ref[...]


def _modulation(c, w_mod, b_mod):
    depth, d, e = w_mod.shape
    bsz = c.shape[0]
    rows = -(-bsz // SUBLANES) * SUBLANES
    c_pad = jnp.zeros((rows, d), F32).at[:bsz].set(c)
    out = pl.pallas_call(
        _mod_kernel,
        grid=(depth, e // MOD_COLS),
        in_specs=[
            pl.BlockSpec((rows, d), lambda i, j: (0, 0)),
            pl.BlockSpec((None, d, MOD_COLS), lambda i, j: (i, 0, j)),
            pl.BlockSpec((None, 1, MOD_COLS), lambda i, j: (i, 0, j)),
        ],
        out_specs=pl.BlockSpec((None, rows, MOD_COLS), lambda i, j: (i, 0, j)),
        out_shape=jax.ShapeDtypeStruct((depth, rows, e), F32),
        compiler_params=pltpu.CompilerParams(
            dimension_semantics=("arbitrary", "arbitrary"), vmem_limit_bytes=VMEM_LIMIT),
        name="adaln_mod",
    )(c_pad, w_mod, b_mod.reshape(depth, 1, e))
    return out[:, :bsz].reshape(depth, bsz, N_MOD, d)


def _norm_to_slab(x_ref, g, shift, scale, slab):
    d = x_ref.shape[1]
    r_scale = _rms_scale(x_ref[...])
    gs = g * (1.0 + scale)
    yield 256, ()
    for cols in _splits(d, 4):
        h = (x_ref[:, cols] * r_scale) * gs[:, cols] + shift[:, cols]
        for c in range(cols.start // LANES, cols.stop // LANES):
            for r in range(SUBLANES):
                slab[c, r * SEG_PITCH:r * SEG_PITCH + SEG_LEN, :] = (
                    h[r * SEG_LEN:(r + 1) * SEG_LEN,
                      c * LANES - cols.start:(c + 1) * LANES - cols.start])
        yield 100, (slab,)


def _permuted_block(slab, j, c):
    return slab[c, pl.ds(j, SUBLANES, stride=SEG_PITCH), :]


def _shift_segments(prev_block, cur_block, row):
    return jnp.where(row == 0, pltpu.roll(prev_block, 1, 0), pltpu.roll(cur_block, 1, 0))


def _store_slab(y, slab, cols):
    for c in range(cols.start // LANES, cols.stop // LANES):
        slab[c, 0:y.shape[0], :] = y[:, c * LANES - cols.start:(c + 1) * LANES - cols.start]


def _sigmoid_of_twice(v):
    return 0.5 * jnp.tanh(v) + 0.5


def _gelu_tanh(v):
    return (0.5 * v) * (1.0 + jnp.tanh(v * (GELU_C1 + GELU_C2 * (v * v))))


def _lru_reset(scratch):
    hp_buf, xbuf, xc_buf, mm_buf, a_buf, u_buf, start_buf, h_carry = scratch
    ts, width = a_buf.shape
    xbuf[ts:ts + CONV_HALO * SUBLANES, :] = jnp.zeros((CONV_HALO * SUBLANES, width), F32)
    h_carry[...] = jnp.zeros_like(h_carry)


def _lru_phases(gate, slab, weights, scratch):
    wy_ref, by_ref, win_ref, bin_ref, cw_ref, cb_ref, wax_ref, bax_ref, lam_ref, wout_ref, \
        bout_ref = weights
    hp_buf, xbuf, xc_buf, mm_buf, a_buf, u_buf, start_buf, h_carry = scratch
    ts, width = a_buf.shape
    d = hp_buf.shape[1]
    heads, hd, _ = wax_ref.shape
    halo = CONV_HALO * SUBLANES
    pair = 2 * SUBLANES

    for js in _splits(SEG_LEN, 2):
        for j in range(js.start, js.stop, 2):
            for c in range(d // LANES):
                both = jnp.concatenate(
                    [_permuted_block(slab, j, c), _permuted_block(slab, j + 1, c)], axis=0)
                hp_buf[j * SUBLANES:j * SUBLANES + pair, c * LANES:(c + 1) * LANES] = (
                    both.astype(BF16))
        yield 100, (hp_buf,)

    for cols in _splits(width, 2):
        n = cols.stop - cols.start
        row = lax.broadcasted_iota(jnp.int32, (SUBLANES, n), 0)
        prev_tail = xbuf[ts:ts + halo, cols]
        xr = jnp.dot(hp_buf[...], win_ref[:, cols], preferred_element_type=F32) + bin_ref[:, cols]
        xbuf[halo:halo + ts, cols] = xr
        for p in range(CONV_HALO):
            blk = slice(p * SUBLANES, (p + 1) * SUBLANES)
            cur = xr[ts - halo + p * SUBLANES:ts - halo + (p + 1) * SUBLANES, :]
            xbuf[blk, cols] = _shift_segments(prev_tail[blk, :], cur, row)
        yield 0, (xbuf,)
    for cols in _splits(width, 4):
        part = cb_ref[:, cols]
        for k in range(CONV_WIDTH):
            part = part + xbuf[k * SUBLANES:k * SUBLANES + ts, cols] * cw_ref[k:k + 1, cols]
        xc_buf[:, cols] = part
        yield 256, (xc_buf,)

    log_sig = LRU_C * jax.nn.log_sigmoid(lam_ref[...])

    def gate_matmul(hh):
        park = slice((hh % 2) * 2 * hd, (hh % 2 + 1) * 2 * hd)
        mm_buf[:, park] = jnp.dot(xc_buf[:, hh * hd:(hh + 1) * hd].astype(BF16), wax_ref[hh],
                                  preferred_element_type=F32) + bax_ref(hh)

    def gate_math(hh, sub, rows):
        base = (hh % 2) * 2 * hd
        cols = slice(hh * hd + sub.start, hh * hd + sub.stop)
        gate_r = _sigmoid_of_twice(mm_buf[rows, base + sub.start:base + sub.stop])
        gate_i = _sigmoid_of_twice(mm_buf[rows, base + hd + sub.start:base + hd + sub.stop])
        log_a = gate_r * log_sig[:, cols]
        a_buf[rows, cols] = jnp.exp(log_a)
        t = jnp.tanh(log_a)
        q = -2.0 * t
        mult = jnp.where(q > 0.0, q * lax.rsqrt(q * (1.0 - t)), 0.0)
        u_buf[rows, cols] = mult * (gate_i * xc_buf[rows, cols])

    for first in range(0, heads, 2):
        for hh in range(first, min(first + 2, heads)):
            gate_matmul(hh)
            yield 0, (mm_buf,)
        for hh in range(first, min(first + 2, heads)):
            for sub in _splits(hd, hd // LANES):
                for rows in _splits(ts, 2):
                    gate_math(hh, sub, rows)
                    yield 240, (a_buf, u_buf)

    for cols in _splits(width, 2):
        n = cols.stop - cols.start
        row = lax.broadcasted_iota(jnp.int32, (SUBLANES, n), 0)
        seg_u = jnp.zeros((SUBLANES, n), F32)
        seg_a = jnp.ones((SUBLANES, n), F32)
        for j in range(SEG_LEN):
            blk = slice(j * SUBLANES, (j + 1) * SUBLANES)
            a = a_buf[blk, cols]
            seg_u = a * seg_u + u_buf[blk, cols]
            seg_a = a * seg_a
            u_buf[blk, cols] = seg_u
            a_buf[blk, cols] = seg_a
        k = 1
        while k < SUBLANES:
            keep = row >= k
            a_prev = jnp.where(keep, pltpu.roll(seg_a, k, 0), 1.0)
            u_prev = jnp.where(keep, pltpu.roll(seg_u, k, 0), 0.0)
            seg_u = seg_u + seg_a * u_prev
            seg_a = seg_a * a_prev
            k *= 2
        state_in = h_carry[0:1, cols]
        seg_end = seg_a * state_in + seg_u
        h_carry[0:1, cols] = seg_end[SUBLANES - 1:SUBLANES, :]
        start_buf[:, cols] = jnp.where(row == 0, state_in, pltpu.roll(seg_end, 1, 0))
        yield 200, (a_buf, u_buf)

    for cols in _splits(width, 2):
        mm_buf[:, cols] = (jnp.dot(hp_buf[...], wy_ref[:, cols], preferred_element_type=F32)
                           + by_ref[:, cols])
        yield 0, (mm_buf,)
    for cols in _splits(width, width // LANES):
        n = cols.stop - cols.start
        hs = (u_buf[:, cols].reshape(SEG_LEN, SUBLANES, n)
              + a_buf[:, cols].reshape(SEG_LEN, SUBLANES, n) * start_buf[:, cols][None]
              ).reshape(ts, n)
        hp_buf[:, cols] = (hs * _gelu_tanh(mm_buf[:, cols])).astype(BF16)
        yield 180, (hp_buf,)

    for cols in _splits(d, 2):
        y = gate[:, cols] * (jnp.dot(hp_buf[...], wout_ref[:, cols],
                                     preferred_element_type=F32) + bout_ref[:, cols])
        _store_slab(y, slab, cols)
        yield 0, (slab,)


def _pool_reset(scratch):
    (hbuf,) = scratch
    ts = hbuf.shape[0] - POOL_HALO * SUBLANES
    hbuf[ts:, :] = jnp.zeros((POOL_HALO * SUBLANES, hbuf.shape[1]), F32)


def _pool_phases(gate, slab, weights, scratch, tile_start):
    pw_ref, ps_ref = weights
    (hbuf,) = scratch
    halo = POOL_HALO * SUBLANES
    ts = hbuf.shape[0] - halo
    d = hbuf.shape[1]
    groups, gd, _ = pw_ref.shape
    row = lax.broadcasted_iota(jnp.int32, (SUBLANES, LANES), 0)

    order = list(range(SEG_LEN - POOL_HALO, SEG_LEN)) + list(range(SEG_LEN - POOL_HALO))
    for js in _splits(SEG_LEN, 2):
        for j in order[js]:
            for c in range(d // LANES):
                lanes = slice(c * LANES, (c + 1) * LANES)
                dst = slice(halo + j * SUBLANES, halo + (j + 1) * SUBLANES)
                new = _permuted_block(slab, j, c)
                if j >= SEG_LEN - POOL_HALO:
                    p = j - (SEG_LEN - POOL_HALO)
                    hbuf[p * SUBLANES:(p + 1) * SUBLANES, lanes] = _shift_segments(
                        hbuf[dst, lanes], new, row)
                hbuf[dst, lanes] = new
        yield 100, (hbuf,)

    m = lax.broadcasted_iota(jnp.int32, (ts, 1), 0)
    pos = tile_start + (m % SUBLANES) * SEG_LEN + m // SUBLANES + 1
    for gi, win in enumerate(POOL_WINDOWS):
        cols = slice(gi * gd, (gi + 1) * gd)
        total = hbuf[(POOL_HALO - (win - 1)) * SUBLANES:, cols]
        step = 1
        while step < win:
            total = total[step * SUBLANES:, :] + total[:-step * SUBLANES, :]
            step *= 2
        inv_count = 1.0 / jnp.minimum(pos, win).astype(F32)
        pooled = total * inv_count - hbuf[halo:, cols]
        mixed = jnp.dot(pooled.astype(BF16), pw_ref[gi], preferred_element_type=F32)
        _store_slab(gate[:, cols] * (mixed * ps_ref[:, cols]), slab, cols)
        yield 250, (slab,)


def _lru_costs(heads, hd, width):
    per_pair = [0, 0] + [240] * (2 * 2 * (hd // LANES))
    return ([256] + [100] * 4 + [100] * 2 + [0] * 2 + [256] * 4 + per_pair * (heads // 2)
            + [200] * 2 + [0] * 2 + [180] * (width // LANES) + [0] * 2)


def _pool_costs(groups):
    return [256] + [100] * 4 + [100] * 2 + [250] * groups


HAND_OVER_COSTS = [130, 130, 190, 190, 190, 190]


def _layer_kernel(*refs, kind, final, tiles_per_seq, n_mixer_weights, n_mixer_scratch):
    x_ref, modm_ref, modf_ref, vec_ref = refs[:4]
    mixer_mats = refs[4:4 + n_mixer_weights]
    w1_ref, w2_ref, o_ref = refs[4 + n_mixer_weights:7 + n_mixer_weights]
    rest = refs[7 + n_mixer_weights:]
    gm_ref, gf_ref = _Rows(vec_ref, 0), _Rows(vec_ref, 1)
    if kind == "lru":
        wy_ref, win_ref, wax_ref, wout_ref = mixer_mats
        heads, hd, _ = wax_ref.shape
        per_row = vec_ref.shape[1] // (2 * hd)
        mixer_weights = (
            wy_ref, _Rows(vec_ref, 2), win_ref, _Rows(vec_ref, 3), _Rows(vec_ref, 4, CONV_WIDTH),
            _Rows(vec_ref, 8), wax_ref,
            lambda hh: vec_ref[11 + hh // per_row:12 + hh // per_row,
                               (hh % per_row) * 2 * hd:(hh % per_row + 1) * 2 * hd],
            _Rows(vec_ref, 9), wout_ref, _Rows(vec_ref, 10))
        fg_ref = _Rows(vec_ref, 13)
    else:
        mixer_weights = (mixer_mats[0], _Rows(vec_ref, 2))
        fg_ref = _Rows(vec_ref, 3)
    slab, x1_buf, h1_buf, h_loc = rest[:4]
    mixer_scratch = rest[4:4 + n_mixer_scratch]
    ts, d = x_ref.shape
    g = pl.program_id(0)

    def run_if(cond, body, trips=1):
        lax.fori_loop(0, jnp.where(cond, trips, 0), lambda i, c: (body(i), c)[1], 0)

    def clear_rows(i):
        pair = 2 * SUBLANES
        rows = pl.ds(pl.multiple_of(i * pair, pair), pair)
        x1_buf[rows, :] = jnp.zeros((pair, d), F32)
        h1_buf[rows, :] = jnp.zeros((pair, d), BF16)

    run_if(g == 0, clear_rows, trips=ts // (2 * SUBLANES))
    run_if(g % tiles_per_seq == 0,
           lambda i: (_lru_reset if kind == "lru" else _pool_reset)(mixer_scratch))

    gate = modm_ref[2:3, :]
    if kind == "lru":
        heads, hd, _ = mixer_weights[6].shape
        assert heads % 2 == 0 and 4 * hd <= d
        mixer = _lru_phases(gate, slab, mixer_weights, mixer_scratch)
        costs = _lru_costs(heads, hd, d)
    else:
        mixer = _pool_phases(gate, slab, mixer_weights, mixer_scratch,
                             (g % tiles_per_seq) * ts)
        costs = _pool_costs(mixer_weights[0].shape[0])
    costs = costs + HAND_OVER_COSTS

    def hand_over(qs):
        for q in qs:
            for r in range(SUBLANES):
                src = q * SUBLANES * SUBLANES + r
                dst = slice(r * SEG_LEN + q * SUBLANES, r * SEG_LEN + (q + 1) * SUBLANES)
                for c in range(d // LANES):
                    lanes = slice(c * LANES, (c + 1) * LANES)
                    x1_buf[dst, lanes] = (
                        x_ref[dst, lanes]
                        + slab[c, pl.ds(src, SUBLANES, stride=SUBLANES), :])

    def all_phases():
        yield from _norm_to_slab(x_ref, gm_ref[...], modm_ref[0:1, :], modm_ref[1:2, :], slab)
        yield from mixer
        for qs in _splits(SEG_LEN // SUBLANES, 2):
            hand_over(range(qs.start, qs.stop))
            yield 130, (x1_buf,)
        gs = gf_ref[...] * (1.0 + modm_ref[4:5, :])
        for rows in _splits(ts, 4):
            x1 = x1_buf[rows, :]
            h1_buf[rows, :] = ((x1 * _rms_scale(x1)) * gs + modm_ref[3:4, :]).astype(BF16)
            yield 190, (h1_buf,)

    phases = all_phases()
    total_cost = float(sum(costs) - sum(HAND_OVER_COSTS))
    spent = 0.0
    n_emitted = 0
    written = []

    def fill(fraction):
        nonlocal spent, n_emitted
        while n_emitted < len(costs) and spent < fraction * total_cost:
            cost, refs = next(phases)
            assert cost == costs[n_emitted], (n_emitted, cost, costs[n_emitted])
            spent += cost
            n_emitted += 1
            written.extend(r for r in refs if all(r is not w for w in written))

    def anchor(target):
        pair = 2 * SUBLANES
        rows = pl.ds(pl.multiple_of(lax.shift_right_logical(g, 30) * pair, pair), pair)
        bits = None
        for ref in written:
            tok = ref[0, rows, :] if len(ref.shape) == 3 else ref[rows, 0:LANES]
            b = pltpu.bitcast(tok, jnp.uint32)
            if b.shape[0] == pair:
                b = b[:SUBLANES] | b[SUBLANES:]
            bits = b if bits is None else bits | b
        del written[:]
        if bits is not None:
            zero = pltpu.bitcast(lax.shift_right_logical(bits, jnp.uint32(32)), BF16)
            target[0:pair, 0:LANES] = target[0:pair, 0:LANES] + zero

    d_ff = w1_ref.shape[1]
    n_chunks = d_ff // FF_CHUNK
    n_gaps = 2 * n_chunks - 1
    acc = None
    for k in range(n_chunks):
        if k > 0:
            anchor(h_loc)
        h_in = h1_buf[...] if k == 0 else h_loc[...]
        cols = slice(k * FF_CHUNK, (k + 1) * FF_CHUNK)
        u = jnp.maximum(jnp.dot(h_in, w1_ref[:, cols], preferred_element_type=F32), 0.0)
        u = (u * u).astype(BF16)
        if k == 0:
            o_ref[...] = x1_buf[...]
            h_loc[...] = h1_buf[...]
        fill((2 * k + 1) / n_gaps)
        p = jnp.dot(u, w2_ref[cols, :], preferred_element_type=F32)
        acc = p if acc is None else acc + p
        fill((2 * k + 2) / n_gaps if k + 1 < n_chunks else 2.0)
    assert n_emitted == len(costs) and next(phases, "done") == "done"
    out = o_ref[...] + modf_ref[5:6, :] * acc
    if final:
        out = (out * _rms_scale(out)) * fg_ref[...]
    o_ref[...] = out


def _layer(x, mod, layer, kind, vec, mixer_mats, j, w1, w2, final):
    n_tiles, ts, d = x.shape
    bsz = mod.shape[1]
    tiles_per_seq = n_tiles // bsz
    last = n_tiles - 1

    def resident(op, idx):
        nd = op.ndim - 1
        return pl.BlockSpec((None,) + op.shape[1:], lambda g: (idx,) + (0,) * nd,
                            pipeline_mode=pl.Buffered(1))

    def mix_tile(g):
        return jnp.minimum(g, last)

    def mlp_tile(g):
        return jnp.maximum(g - 1, 0)

    operands = [x, mod, mod, vec, *mixer_mats, w1, w2]
    in_specs = [
        pl.BlockSpec((None, ts, d), lambda g: (mix_tile(g), 0, 0)),
        pl.BlockSpec((None, None, N_MOD, d),
                     lambda g: (layer, mix_tile(g) // tiles_per_seq, 0, 0)),
        pl.BlockSpec((None, None, N_MOD, d),
                     lambda g: (layer, mlp_tile(g) // tiles_per_seq, 0, 0)),
        resident(vec, j),
        *[resident(op, j) for op in mixer_mats],
        resident(w1, layer), resident(w2, layer),
    ]

    scratch = [
        pltpu.VMEM((d // LANES, SUBLANES * SEG_PITCH, LANES), F32),
        pltpu.VMEM((ts, d), F32),
        pltpu.VMEM((ts, d), BF16),
        pltpu.VMEM((ts, d), BF16),
    ]
    if kind == "lru":
        width = mixer_mats[0].shape[2]
        assert width == d
        mixer_scratch = [
            pltpu.VMEM((ts, d), BF16),
            pltpu.VMEM((ts + CONV_HALO * SUBLANES, width), F32),
            pltpu.VMEM((ts, width), F32),
            pltpu.VMEM((ts, width), F32),
            pltpu.VMEM((ts, width), F32),
            pltpu.VMEM((ts, width), F32),
            pltpu.VMEM((SUBLANES, width), F32),
            pltpu.VMEM((SUBLANES, width), F32),
        ]
    else:
        mixer_scratch = [pltpu.VMEM((ts + POOL_HALO * SUBLANES, d), F32)]

    return pl.pallas_call(
        functools.partial(_layer_kernel, kind=kind, final=final, tiles_per_seq=tiles_per_seq,
                          n_mixer_weights=len(mixer_mats), n_mixer_scratch=len(mixer_scratch)),
        grid=(n_tiles + 1,),
        in_specs=in_specs,
        out_specs=pl.BlockSpec((None, ts, d), lambda g: (mlp_tile(g), 0, 0)),
        out_shape=jax.ShapeDtypeStruct(x.shape, F32),
        scratch_shapes=scratch + mixer_scratch,
        compiler_params=pltpu.CompilerParams(
            dimension_semantics=("arbitrary",), vmem_limit_bytes=VMEM_LIMIT),
        name=kind + ("_layer_final" if final else "_layer"),
    )(*operands)


def kernel(x, c, w_mod, b_mod, norm_mix_g, norm_ffn_g, lru_w_y, lru_b_y, lru_w_in, lru_b_in,
           lru_conv_w, lru_conv_b, lru_w_a, lru_b_a, lru_w_x, lru_b_x, lru_lambda, lru_w_out,
           lru_b_out, pool_w, pool_scale, ffn_w1, ffn_w2, final_norm_g):
    depth = w_mod.shape[0]
    bsz, seq, d = x.shape
    assert seq % SEQ_TILE == 0 and ffn_w1.shape[2] % FF_CHUNK == 0
    assert w_mod.shape[2] % MOD_COLS == 0 and d % (4 * LANES) == 0
    mod = _modulation(c, w_mod, b_mod)

    n_a, heads, hd, _ = lru_w_a.shape
    n_b = pool_w.shape[0]
    w_ax = (0.5 * jnp.concatenate([lru_w_a, lru_w_x], axis=-1)).astype(BF16)
    b_ax = (0.5 * jnp.concatenate([lru_b_a, lru_b_x], axis=-1)).reshape(n_a, -1, d)
    lru_mats = (lru_w_y.astype(BF16), lru_w_in.astype(BF16), w_ax, lru_w_out.astype(BF16))
    pool_mats = (pool_w.astype(BF16),)
    w1, w2 = ffn_w1.astype(BF16), ffn_w2.astype(BF16)

    def rows(v):
        return v[:, None, :]

    def last_norm(n):
        return jnp.broadcast_to(final_norm_g, (n, 1, d))

    assert b_ax.shape[1] == 2
    vec_lru = jnp.concatenate(
        [rows(norm_mix_g[0::N_MIXERS]), rows(norm_ffn_g[0::N_MIXERS]), rows(lru_b_y),
         rows(lru_b_in), lru_conv_w, rows(lru_conv_b), rows(lru_lambda), rows(lru_b_out), b_ax,
         last_norm(n_a), jnp.zeros((n_a, 2, d), F32)], axis=1)
    vec_pool = jnp.concatenate(
        [rows(norm_mix_g[1::N_MIXERS]), rows(norm_ffn_g[1::N_MIXERS]), rows(pool_scale),
         last_norm(n_b)], axis=1)

    x = x.reshape(bsz * seq // SEQ_TILE, SEQ_TILE, d)
    for i in range(depth):
        j = i // N_MIXERS
        if i % N_MIXERS == 0:
            kind, vec, mats = "lru", vec_lru, lru_mats
        else:
            kind, vec, mats = "pool", vec_pool, pool_mats
        x = _layer(x, mod, i, kind, vec, mats, j, w1, w2, i == depth - 1)
    return x.reshape(bsz, seq, d)
```

```python
import functools
import math

import jax
import jax.numpy as jnp
from jax import lax
from jax.experimental import pallas as pl
from jax.experimental.pallas import tpu as pltpu

F32 = jnp.float32
BF16 = jnp.bfloat16

N_MIXERS = 2
N_MOD = 6
CONV_WIDTH = 4
LRU_C = 8.0
POOL_WINDOWS = (2, 4, 8, 16)
EPS = 1e-6

SUBLANES = 8
LANES = 128
SEQ_TILE = 512
SEG_LEN = SEQ_TILE // SUBLANES
SEG_PITCH = SEG_LEN + SUBLANES
FF_CHUNK = 1024
MOD_COLS = 1536
CONV_HALO = CONV_WIDTH - 1
POOL_HALO = max(POOL_WINDOWS)
VMEM_LIMIT = 58 * 1024 * 1024

GELU_C1 = math.sqrt(2.0 / math.pi)
GELU_C2 = GELU_C1 * 0.044715


def _rms_scale(x):
    return lax.rsqrt(jnp.mean(x * x, axis=-1, keepdims=True) + EPS)


def _splits(n, parts):
    step = n // parts
    return tuple(slice(i * step, (i + 1) * step) for i in range(parts))


class _Rows:
    def __init__(self, ref, first, count=1):
        self.ref, self.first, self.count = ref, first, count

    def __getitem__(self, idx):
        if idx is Ellipsis:
            return self.ref[self.first:self.first + self.count, :]
        rows, cols = idx
        start = self.first + (rows.start or 0)
        stop = self.first + (self.count if rows.stop is None else rows.stop)
        return self.ref[start:stop, cols]


def _mod_kernel(c_ref, w_ref, b_ref, o_ref):
    c = c_ref[...]
    cond = c * jax.nn.sigmoid(c)
    o_ref[...] = jnp.dot(cond.astype(BF16), w_ref[...].astype(BF16),
                         preferred_element_type=F32) + b_ref[...]


def _modulation(c, w_mod, b_mod):
    depth, d, e = w_mod.shape
    bsz = c.shape[0]
    rows = -(-bsz // SUBLANES) * SUBLANES
    c_pad = jnp.zeros((rows, d), F32).at[:bsz].set(c)
    out = pl.pallas_call(
        _mod_kernel,
        grid=(depth, e // MOD_COLS),
        in_specs=[
            pl.BlockSpec((rows, d), lambda i, j: (0, 0)),
            pl.BlockSpec((None, d, MOD_COLS), lambda i, j: (i, 0, j)),
            pl.BlockSpec((None, 1, MOD_COLS), lambda i, j: (i, 0, j)),
        ],
        out_specs=pl.BlockSpec((None, rows, MOD_COLS), lambda i, j: (i, 0, j)),
        out_shape=jax.ShapeDtypeStruct((depth, rows, e), F32),
        compiler_params=pltpu.CompilerParams(
            dimension_semantics=("arbitrary", "arbitrary"), vmem_limit_bytes=VMEM_LIMIT),
        name="adaln_mod",
    )(c_pad, w_mod, b_mod.reshape(depth, 1, e))
    return out[:, :bsz].reshape(depth, bsz, N_MOD, d)


def _permute_in(x_ref, slab, xp_buf):
    d = x_ref.shape[1]
    for cols in _splits(d, 4):
        for c in range(cols.start // LANES, cols.stop // LANES):
            for r in range(SUBLANES):
                slab[c, r * SEG_PITCH:r * SEG_PITCH + SEG_LEN, :] = (
                    x_ref[r * SEG_LEN:(r + 1) * SEG_LEN, c * LANES:(c + 1) * LANES])
        yield 0, (slab,)
    for js in _splits(SEG_LEN, 2):
        for j in range(js.start, js.stop):
            for c in range(d // LANES):
                xp_buf[j * SUBLANES:(j + 1) * SUBLANES, c * LANES:(c + 1) * LANES] = (
                    slab[c, pl.ds(j, SUBLANES, stride=SEG_PITCH), :])
        yield 70, (xp_buf,)


def _permute_out(y, slab, o_ref):
    d = y.shape[1]
    for c in range(d // LANES):
        slab[c, 0:y.shape[0], :] = y[:, c * LANES:(c + 1) * LANES]
    for q in range(SEG_LEN // SUBLANES):
        for r in range(SUBLANES):
            src = q * SUBLANES * SUBLANES + r
            dst = slice(r * SEG_LEN + q * SUBLANES, r * SEG_LEN + (q + 1) * SUBLANES)
            for c in range(d // LANES):
                o_ref[dst, c * LANES:(c + 1) * LANES] = (
                    slab[c, pl.ds(src, SUBLANES, stride=SUBLANES), :])


def _shift_segments(prev_block, cur_block, row):
    return jnp.where(row == 0, pltpu.roll(prev_block, 1, 0), pltpu.roll(cur_block, 1, 0))


def _sigmoid_of_twice(v):
    return 0.5 * jnp.tanh(v) + 0.5


def _gelu_tanh(v):
    return (0.5 * v) * (1.0 + jnp.tanh(v * (GELU_C1 + GELU_C2 * (v * v))))


def _lru_reset(scratch):
    hp_buf, xbuf, xc_buf, mm_buf, a_buf, u_buf, start_buf, h_carry = scratch
    ts, width = a_buf.shape
    xbuf[ts:ts + CONV_HALO * SUBLANES, :] = jnp.zeros((CONV_HALO * SUBLANES, width), F32)
    h_carry[...] = jnp.zeros_like(h_carry)


def _lru_phases(xp, g, shift, scale, gate, x1_buf, weights, scratch):
    wy_ref, by_ref, win_ref, bin_ref, cw_ref, cb_ref, wax_ref, bax_ref, lam_ref, wout_ref, \
        bout_ref = weights
    hp_buf, xbuf, xc_buf, mm_buf, a_buf, u_buf, start_buf, h_carry = scratch
    ts, width = a_buf.shape
    d = hp_buf.shape[1]
    heads, hd, _ = wax_ref.shape
    halo = CONV_HALO * SUBLANES

    r_scale = _rms_scale(xp[...])
    gs = g * (1.0 + scale)
    yield 256, ()
    for cols in _splits(d, 4):
        hp_buf[:, cols] = ((xp[:, cols] * r_scale) * gs[:, cols] + shift[:, cols]).astype(BF16)
        yield 100, (hp_buf,)

    for cols in _splits(width, 2):
        n = cols.stop - cols.start
        row = lax.broadcasted_iota(jnp.int32, (SUBLANES, n), 0)
        prev_tail = xbuf[ts:ts + halo, cols]
        xr = jnp.dot(hp_buf[...], win_ref[:, cols], preferred_element_type=F32) + bin_ref[:, cols]
        xbuf[halo:halo + ts, cols] = xr
        for p in range(CONV_HALO):
            blk = slice(p * SUBLANES, (p + 1) * SUBLANES)
            cur = xr[ts - halo + p * SUBLANES:ts - halo + (p + 1) * SUBLANES, :]
            xbuf[blk, cols] = _shift_segments(prev_tail[blk, :], cur, row)
        yield 0, (xbuf,)
    for cols in _splits(width, 4):
        part = cb_ref[:, cols]
        for k in range(CONV_WIDTH):
            part = part + xbuf[k * SUBLANES:k * SUBLANES + ts, cols] * cw_ref[k:k + 1, cols]
        xc_buf[:, cols] = part
        yield 256, (xc_buf,)

    log_sig = LRU_C * jax.nn.log_sigmoid(lam_ref[...])

    def gate_matmul(hh):
        park = slice((hh % 2) * 2 * hd, (hh % 2 + 1) * 2 * hd)
        mm_buf[:, park] = jnp.dot(xc_buf[:, hh * hd:(hh + 1) * hd].astype(BF16), wax_ref[hh],
                                  preferred_element_type=F32) + bax_ref(hh)

    def gate_math(hh, sub, rows):
        base = (hh % 2) * 2 * hd
        cols = slice(hh * hd + sub.start, hh * hd + sub.stop)
        gate_r = _sigmoid_of_twice(mm_buf[rows, base + sub.start:base + sub.stop])
        gate_i = _sigmoid_of_twice(mm_buf[rows, base + hd + sub.start:base + hd + sub.stop])
        log_a = gate_r * log_sig[:, cols]
        a_buf[rows, cols] = jnp.exp(log_a)
        t = jnp.tanh(log_a)
        q = -2.0 * t
        mult = jnp.where(q > 0.0, q * lax.rsqrt(q * (1.0 - t)), 0.0)
        u_buf[rows, cols] = mult * (gate_i * xc_buf[rows, cols])

    for first in range(0, heads, 2):
        for hh in range(first, min(first + 2, heads)):
            gate_matmul(hh)
            yield 0, (mm_buf,)
        for hh in range(first, min(first + 2, heads)):
            for sub in _splits(hd, hd // LANES):
                for rows in _splits(ts, 2):
                    gate_math(hh, sub, rows)
                    yield 240, (a_buf, u_buf)

    for cols in _splits(width, 2):
        n = cols.stop - cols.start
        row = lax.broadcasted_iota(jnp.int32, (SUBLANES, n), 0)
        seg_u = jnp.zeros((SUBLANES, n), F32)
        seg_a = jnp.ones((SUBLANES, n), F32)
        for j in range(SEG_LEN):
            blk = slice(j * SUBLANES, (j + 1) * SUBLANES)
            a = a_buf[blk, cols]
            seg_u = a * seg_u + u_buf[blk, cols]
            seg_a = a * seg_a
            u_buf[blk, cols] = seg_u
            a_buf[blk, cols] = seg_a
        k = 1
        while k < SUBLANES:
            keep = row >= k
            a_prev = jnp.where(keep, pltpu.roll(seg_a, k, 0), 1.0)
            u_prev = jnp.where(keep, pltpu.roll(seg_u, k, 0), 0.0)
            seg_u = seg_u + seg_a * u_prev
            seg_a = seg_a * a_prev
            k *= 2
        state_in = h_carry[0:1, cols]
        seg_end = seg_a * state_in + seg_u
        h_carry[0:1, cols] = seg_end[SUBLANES - 1:SUBLANES, :]
        start_buf[:, cols] = jnp.where(row == 0, state_in, pltpu.roll(seg_end, 1, 0))
        yield 200, (a_buf, u_buf)

    for cols in _splits(width, 2):
        mm_buf[:, cols] = (jnp.dot(hp_buf[...], wy_ref[:, cols], preferred_element_type=F32)
                           + by_ref[:, cols])
        yield 0, (mm_buf,)
    for cols in _splits(width, width // LANES):
        n = cols.stop - cols.start
        hs = (u_buf[:, cols].reshape(SEG_LEN, SUBLANES, n)
              + a_buf[:, cols].reshape(SEG_LEN, SUBLANES, n) * start_buf[:, cols][None]
              ).reshape(ts, n)
        hp_buf[:, cols] = (hs * _gelu_tanh(mm_buf[:, cols])).astype(BF16)
        yield 180, (hp_buf,)

    for cols in _splits(d, 2):
        y = jnp.dot(hp_buf[...], wout_ref[:, cols], preferred_element_type=F32) + bout_ref[:, cols]
        x1_buf[:, cols] = xp[:, cols] + gate[:, cols] * y
        yield 64, (x1_buf,)


def _lru_costs(heads, hd, width):
    per_pair = [0, 0] + [240] * (2 * 2 * (hd // LANES))
    return ([256] + [100] * 4 + [0] * 2 + [256] * 4 + per_pair * (heads // 2)
            + [200] * 2 + [0] * 2 + [180] * (width // LANES) + [64] * 2)


def _pool_reset(scratch):
    (hbuf,) = scratch
    ts = hbuf.shape[0] - POOL_HALO * SUBLANES
    hbuf[ts:, :] = jnp.zeros((POOL_HALO * SUBLANES, hbuf.shape[1]), F32)


def _pool_phases(xp, g, shift, scale, gate, x1_buf, weights, scratch, tile_start):
    pw_ref, ps_ref = weights
    (hbuf,) = scratch
    halo = POOL_HALO * SUBLANES
    ts = hbuf.shape[0] - halo
    d = hbuf.shape[1]
    groups, gd, _ = pw_ref.shape

    r_scale = _rms_scale(xp[...])
    gs = g * (1.0 + scale)
    yield 256, ()
    for cols in _splits(d, 4):
        n = cols.stop - cols.start
        row = lax.broadcasted_iota(jnp.int32, (SUBLANES, n), 0)
        prev_tail = hbuf[ts:ts + halo, cols]
        h = (xp[:, cols] * r_scale) * gs[:, cols] + shift[:, cols]
        hbuf[halo:halo + ts, cols] = h
        for p in range(POOL_HALO):
            blk = slice(p * SUBLANES, (p + 1) * SUBLANES)
            cur = h[ts - halo + p * SUBLANES:ts - halo + (p + 1) * SUBLANES, :]
            hbuf[blk, cols] = _shift_segments(prev_tail[blk, :], cur, row)
        yield 120, (hbuf,)

    m = lax.broadcasted_iota(jnp.int32, (ts, 1), 0)
    pos = tile_start + (m % SUBLANES) * SEG_LEN + m // SUBLANES + 1
    for gi, win in enumerate(POOL_WINDOWS):
        cols = slice(gi * gd, (gi + 1) * gd)
        total = hbuf[(POOL_HALO - (win - 1)) * SUBLANES:, cols]
        step = 1
        while step < win:
            total = total[step * SUBLANES:, :] + total[:-step * SUBLANES, :]
            step *= 2
        inv_count = 1.0 / jnp.minimum(pos, win).astype(F32)
        pooled = total * inv_count - hbuf[halo:, cols]
        mixed = jnp.dot(pooled.astype(BF16), pw_ref[gi], preferred_element_type=F32)
        x1_buf[:, cols] = xp[:, cols] + gate[:, cols] * (mixed * ps_ref[:, cols])
        yield 280, (x1_buf,)


def _pool_costs(groups):
    return [256] + [120] * 4 + [280] * groups


PERMUTE_IN_COSTS = [0] * 4 + [70] * 2
HAND_OVER_COSTS = [190, 190, 190, 190]


def _layer_kernel(*refs, kind, first, final, tiles_per_seq, n_mixer_weights,
                  n_mixer_scratch):
    x_ref, modm_ref, modf_ref, vec_ref = refs[:4]
    mixer_mats = refs[4:4 + n_mixer_weights]
    w1_ref, w2_ref, o_ref = refs[4 + n_mixer_weights:7 + n_mixer_weights]
    rest = refs[7 + n_mixer_weights:]
    gm_ref, gf_ref = _Rows(vec_ref, 0), _Rows(vec_ref, 1)
    if kind == "lru":
        wy_ref, win_ref, wax_ref, wout_ref = mixer_mats
        heads, hd, _ = wax_ref.shape
        per_row = vec_ref.shape[1] // (2 * hd)
        mixer_weights = (
            wy_ref, _Rows(vec_ref, 2), win_ref, _Rows(vec_ref, 3), _Rows(vec_ref, 4, CONV_WIDTH),
            _Rows(vec_ref, 8), wax_ref,
            lambda hh: vec_ref[11 + hh // per_row:12 + hh // per_row,
                               (hh % per_row) * 2 * hd:(hh % per_row + 1) * 2 * hd],
            _Rows(vec_ref, 9), wout_ref, _Rows(vec_ref, 10))
        fg_ref = _Rows(vec_ref, 13)
    else:
        mixer_weights = (mixer_mats[0], _Rows(vec_ref, 2))
        fg_ref = _Rows(vec_ref, 3)
    x1_buf, h1_buf, h_loc = rest[:3]
    rest = rest[3:]
    slab = xp_buf = None
    if first or final:
        slab, rest = rest[0], rest[1:]
    if first:
        xp_buf, rest = rest[0], rest[1:]
    mixer_scratch = rest[:n_mixer_scratch]
    ts, d = x_ref.shape
    g = pl.program_id(0)

    def run_if(cond, body, trips=1):
        lax.fori_loop(0, jnp.where(cond, trips, 0), lambda i, c: (body(i), c)[1], 0)

    def clear_rows(i):
        pair = 2 * SUBLANES
        rows = pl.ds(pl.multiple_of(i * pair, pair), pair)
        x1_buf[rows, :] = jnp.zeros((pair, d), F32)
        h1_buf[rows, :] = jnp.zeros((pair, d), BF16)

    run_if(g == 0, clear_rows, trips=ts // (2 * SUBLANES))
    run_if(g % tiles_per_seq == 0,
           lambda i: (_lru_reset if kind == "lru" else _pool_reset)(mixer_scratch))

    xp = xp_buf if first else x_ref
    norm_args = (xp, gm_ref[...], modm_ref[0:1, :], modm_ref[1:2, :], modm_ref[2:3, :], x1_buf)
    if kind == "lru":
        heads, hd, _ = mixer_weights[6].shape
        assert heads % 2 == 0 and 4 * hd <= d
        mixer = _lru_phases(*norm_args, mixer_weights, mixer_scratch)
        costs = _lru_costs(heads, hd, d)
    else:
        mixer = _pool_phases(*norm_args, mixer_weights, mixer_scratch,
                             (g % tiles_per_seq) * ts)
        costs = _pool_costs(mixer_weights[0].shape[0])
    costs = (PERMUTE_IN_COSTS if first else []) + costs + HAND_OVER_COSTS

    def all_phases():
        if first:
            yield from _permute_in(x_ref, slab, xp_buf)
        yield from mixer
        gs = gf_ref[...] * (1.0 + modm_ref[4:5, :])
        for rows in _splits(ts, 4):
            x1 = x1_buf[rows, :]
            h1_buf[rows, :] = ((x1 * _rms_scale(x1)) * gs + modm_ref[3:4, :]).astype(BF16)
            yield 190, (h1_buf,)

    phases = all_phases()
    total_cost = float(sum(costs) - sum(HAND_OVER_COSTS))
    spent = 0.0
    n_emitted = 0
    written = []

    def fill(fraction):
        nonlocal spent, n_emitted
        while n_emitted < len(costs) and spent < fraction * total_cost:
            cost, refs = next(phases)
            assert cost == costs[n_emitted], (n_emitted, cost, costs[n_emitted])
            spent += cost
            n_emitted += 1
            written.extend(r for r in refs if all(r is not w for w in written))

    def anchor(target):
        pair = 2 * SUBLANES
        rows = pl.ds(pl.multiple_of(lax.shift_right_logical(g, 30) * pair, pair), pair)
        bits = None
        for ref in written:
            tok = ref[0, rows, :] if len(ref.shape) == 3 else ref[rows, 0:LANES]
            b = pltpu.bitcast(tok, jnp.uint32)
            if b.shape[0] == pair:
                b = b[:SUBLANES] | b[SUBLANES:]
            bits = b if bits is None else bits | b
        del written[:]
        if bits is not None:
            zero = pltpu.bitcast(lax.shift_right_logical(bits, jnp.uint32(32)), BF16)
            target[0:pair, 0:LANES] = target[0:pair, 0:LANES] + zero

    d_ff = w1_ref.shape[1]
    n_chunks = d_ff // FF_CHUNK
    n_gaps = 2 * n_chunks - 1
    acc = None
    for k in range(n_chunks):
        if k > 0:
            anchor(h_loc)
        h_in = h1_buf[...] if k == 0 else h_loc[...]
        cols = slice(k * FF_CHUNK, (k + 1) * FF_CHUNK)
        u = jnp.maximum(jnp.dot(h_in, w1_ref[:, cols], preferred_element_type=F32), 0.0)
        u = (u * u).astype(BF16)
        if k == 0:
            o_ref[...] = x1_buf[...]
            h_loc[...] = h1_buf[...]
        fill((2 * k + 1) / n_gaps)
        p = jnp.dot(u, w2_ref[cols, :], preferred_element_type=F32)
        acc = p if acc is None else acc + p
        fill((2 * k + 2) / n_gaps if k + 1 < n_chunks else 2.0)
    assert n_emitted == len(costs) and next(phases, "done") == "done"
    out = o_ref[...] + modf_ref[5:6, :] * acc
    if final:
        out = (out * _rms_scale(out)) * fg_ref[...]
        _permute_out(out, slab, o_ref)
    else:
        o_ref[...] = out


def _layer(x, mod, layer, kind, vec, mixer_mats, j, w1, w2, first, final):
    n_tiles, ts, d = x.shape
    bsz = mod.shape[1]
    tiles_per_seq = n_tiles // bsz
    last = n_tiles - 1

    def resident(op, idx):
        nd = op.ndim - 1
        return pl.BlockSpec((None,) + op.shape[1:], lambda g: (idx,) + (0,) * nd,
                            pipeline_mode=pl.Buffered(1))

    def mix_tile(g):
        return jnp.minimum(g, last)

    def mlp_tile(g):
        return jnp.maximum(g - 1, 0)

    operands = [x, mod, mod, vec, *mixer_mats, w1, w2]
    in_specs = [
        pl.BlockSpec((None, ts, d), lambda g: (mix_tile(g), 0, 0)),
        pl.BlockSpec((None, None, N_MOD, d),
                     lambda g: (layer, mix_tile(g) // tiles_per_seq, 0, 0)),
        pl.BlockSpec((None, None, N_MOD, d),
                     lambda g: (layer, mlp_tile(g) // tiles_per_seq, 0, 0)),
        resident(vec, j),
        *[resident(op, j) for op in mixer_mats],
        resident(w1, layer), resident(w2, layer),
    ]

    scratch = [
        pltpu.VMEM((ts, d), F32),
        pltpu.VMEM((ts, d), BF16),
        pltpu.VMEM((ts, d), BF16),
    ]
    if first or final:
        scratch.append(pltpu.VMEM((d // LANES, SUBLANES * SEG_PITCH, LANES), F32))
    if first:
        scratch.append(pltpu.VMEM((ts, d), F32))
    if kind == "lru":
        width = mixer_mats[0].shape[2]
        assert width == d
        mixer_scratch = [
            pltpu.VMEM((ts, d), BF16),
            pltpu.VMEM((ts + CONV_HALO * SUBLANES, width), F32),
            pltpu.VMEM((ts, width), F32),
            pltpu.VMEM((ts, width), F32),
            pltpu.VMEM((ts, width), F32),
            pltpu.VMEM((ts, width), F32),
            pltpu.VMEM((SUBLANES, width), F32),
            pltpu.VMEM((SUBLANES, width), F32),
        ]
    else:
        mixer_scratch = [pltpu.VMEM((ts + POOL_HALO * SUBLANES, d), F32)]

    return pl.pallas_call(
        functools.partial(_layer_kernel, kind=kind, first=first, final=final,
                          tiles_per_seq=tiles_per_seq, n_mixer_weights=len(mixer_mats),
                          n_mixer_scratch=len(mixer_scratch)),
        grid=(n_tiles + 1,),
        in_specs=in_specs,
        out_specs=pl.BlockSpec((None, ts, d), lambda g: (mlp_tile(g), 0, 0)),
        out_shape=jax.ShapeDtypeStruct(x.shape, F32),
        scratch_shapes=scratch + mixer_scratch,
        compiler_params=pltpu.CompilerParams(
            dimension_semantics=("arbitrary",), vmem_limit_bytes=VMEM_LIMIT),
        name=kind + "_layer" + ("_first" if first else "") + ("_final" if final else ""),
    )(*operands)


def kernel(x, c, w_mod, b_mod, norm_mix_g, norm_ffn_g, lru_w_y, lru_b_y, lru_w_in, lru_b_in,
           lru_conv_w, lru_conv_b, lru_w_a, lru_b_a, lru_w_x, lru_b_x, lru_lambda, lru_w_out,
           lru_b_out, pool_w, pool_scale, ffn_w1, ffn_w2, final_norm_g):
    depth = w_mod.shape[0]
    bsz, seq, d = x.shape
    assert seq % SEQ_TILE == 0 and ffn_w1.shape[2] % FF_CHUNK == 0
    assert w_mod.shape[2] % MOD_COLS == 0 and d % (4 * LANES) == 0
    mod = _modulation(c, w_mod, b_mod)

    n_a, heads, hd, _ = lru_w_a.shape
    n_b = pool_w.shape[0]
    w_ax = (0.5 * jnp.concatenate([lru_w_a, lru_w_x], axis=-1)).astype(BF16)
    b_ax = (0.5 * jnp.concatenate([lru_b_a, lru_b_x], axis=-1)).reshape(n_a, -1, d)
    lru_mats = (lru_w_y.astype(BF16), lru_w_in.astype(BF16), w_ax, lru_w_out.astype(BF16))
    pool_mats = (pool_w.astype(BF16),)
    w1, w2 = ffn_w1.astype(BF16), ffn_w2.astype(BF16)

    def rows(v):
        return v[:, None, :]

    def last_norm(n):
        return jnp.broadcast_to(final_norm_g, (n, 1, d))

    assert b_ax.shape[1] == 2
    vec_lru = jnp.concatenate(
        [rows(norm_mix_g[0::N_MIXERS]), rows(norm_ffn_g[0::N_MIXERS]), rows(lru_b_y),
         rows(lru_b_in), lru_conv_w, rows(lru_conv_b), rows(lru_lambda), rows(lru_b_out), b_ax,
         last_norm(n_a), jnp.zeros((n_a, 2, d), F32)], axis=1)
    vec_pool = jnp.concatenate(
        [rows(norm_mix_g[1::N_MIXERS]), rows(norm_ffn_g[1::N_MIXERS]), rows(pool_scale),
         last_norm(n_b)], axis=1)

    x = x.reshape(bsz * seq // SEQ_TILE, SEQ_TILE, d)
    for i in range(depth):
        j = i // N_MIXERS
        if i % N_MIXERS == 0:
            kind, vec, mats = "lru", vec_lru, lru_mats
        else:
            kind, vec, mats = "pool", vec_pool, pool_mats
        x = _layer(x, mod, i, kind, vec, mats, j, w1, w2, i == 0, i == depth - 1)
    return x.reshape(bsz, seq, d)
```

```python
import functools
import math

import jax
import jax.numpy as jnp
from jax import lax
from jax.experimental import pallas as pl
from jax.experimental.pallas import tpu as pltpu

F32 = jnp.float32
BF16 = jnp.bfloat16

N_MIXERS = 2
N_MOD = 6
CONV_WIDTH = 4
LRU_C = 8.0
POOL_WINDOWS = (2, 4, 8, 16)
EPS = 1e-6

SUBLANES = 8
LANES = 128
SEQ_TILE = 512
SEG_LEN = SEQ_TILE // SUBLANES
SEG_PITCH = SEG_LEN + SUBLANES
FF_CHUNK = 1024
MOD_COLS = 1536
CONV_HALO = CONV_WIDTH - 1
POOL_HALO = max(POOL_WINDOWS)
VMEM_LIMIT = 58 * 1024 * 1024

GELU_C1 = math.sqrt(2.0 / math.pi)
GELU_C2 = GELU_C1 * 0.044715


def _rms_scale(x):
    return lax.rsqrt(jnp.mean(x * x, axis=-1, keepdims=True) + EPS)


def _splits(n, parts):
    step = n // parts
    return tuple(slice(i * step, (i + 1) * step) for i in range(parts))


class _Rows:
    def __init__(self, ref, first, count=1):
        self.ref, self.first, self.count = ref, first, count

    def __getitem__(self, idx):
        if idx is Ellipsis:
            return self.ref[self.first:self.first + self.count, :]
        rows, cols = idx
        start = self.first + (rows.start or 0)
        stop = self.first + (self.count if rows.stop is None else rows.stop)
        return self.ref[start:stop, cols]


def _mod_kernel(c_ref, w_ref, b_ref, o_ref):
    c = c_ref[...]
    cond = c * jax.nn.sigmoid(c)
    o_ref[...] = jnp.dot(cond.astype(BF16), w_ref[...].astype(BF16),
                         preferred_element_type=F32) + b_ref[...]


def _modulation(c, w_mod, b_mod):
    depth, d, e = w_mod.shape
    bsz = c.shape[0]
    rows = -(-bsz // SUBLANES) * SUBLANES
    c_pad = jnp.zeros((rows, d), F32).at[:bsz].set(c)
    out = pl.pallas_call(
        _mod_kernel,
        grid=(depth, e // MOD_COLS),
        in_specs=[
            pl.BlockSpec((rows, d), lambda i, j: (0, 0)),
            pl.BlockSpec((None, d, MOD_COLS), lambda i, j: (i, 0, j)),
            pl.BlockSpec((None, 1, MOD_COLS), lambda i, j: (i, 0, j)),
        ],
        out_specs=pl.BlockSpec((None, rows, MOD_COLS), lambda i, j: (i, 0, j)),
        out_shape=jax.ShapeDtypeStruct((depth, rows, e), F32),
        compiler_params=pltpu.CompilerParams(
            dimension_semantics=("arbitrary", "arbitrary"), vmem_limit_bytes=VMEM_LIMIT),
        name="adaln_mod",
    )(c_pad, w_mod, b_mod.reshape(depth, 1, e))
    return out[:, :bsz].reshape(depth, bsz, N_MOD, d)


def _permute_in(x_ref, slab, xp_buf):
    d = x_ref.shape[1]
    for cols in _splits(d, 4):
        for c in range(cols.start // LANES, cols.stop // LANES):
            for r in range(SUBLANES):
                slab[c, r * SEG_PITCH:r * SEG_PITCH + SEG_LEN, :] = (
                    x_ref[r * SEG_LEN:(r + 1) * SEG_LEN, c * LANES:(c + 1) * LANES])
        yield 0, (slab,)
    for js in _splits(SEG_LEN, 2):
        for j in range(js.start, js.stop):
            for c in range(d // LANES):
                xp_buf[j * SUBLANES:(j + 1) * SUBLANES, c * LANES:(c + 1) * LANES] = (
                    slab[c, pl.ds(j, SUBLANES, stride=SEG_PITCH), :])
        yield 70, (xp_buf,)


def _permute_out(y, slab, o_ref):
    d = y.shape[1]
    for c in range(d // LANES):
        slab[c, 0:y.shape[0], :] = y[:, c * LANES:(c + 1) * LANES]
    for q in range(SEG_LEN // SUBLANES):
        for r in range(SUBLANES):
            src = q * SUBLANES * SUBLANES + r
            dst = slice(r * SEG_LEN + q * SUBLANES, r * SEG_LEN + (q + 1) * SUBLANES)
            for c in range(d // LANES):
                o_ref[dst, c * LANES:(c + 1) * LANES] = (
                    slab[c, pl.ds(src, SUBLANES, stride=SUBLANES), :])


def _shift_segments(prev_block, cur_block, row):
    return jnp.where(row == 0, pltpu.roll(prev_block, 1, 0), pltpu.roll(cur_block, 1, 0))


def _sigmoid_of_twice(v):
    return 0.5 * jnp.tanh(v) + 0.5


def _gelu_tanh(v):
    return (0.5 * v) * (1.0 + jnp.tanh(v * (GELU_C1 + GELU_C2 * (v * v))))


def _lru_reset(scratch):
    hp_buf, xbuf, xc_buf, mm_buf, a_buf, u_buf, start_buf, h_carry = scratch
    ts, width = a_buf.shape
    xbuf[ts:ts + CONV_HALO * SUBLANES, :] = jnp.zeros((CONV_HALO * SUBLANES, width), F32)
    h_carry[...] = jnp.zeros_like(h_carry)


def _lru_phases(xp, g, shift, scale, gate, x1_buf, weights, scratch):
    wy_ref, by_ref, win_ref, bin_ref, cw_ref, cb_ref, wax_ref, bax_ref, lam_ref, wout_ref, \
        bout_ref = weights
    hp_buf, xbuf, xc_buf, mm_buf, a_buf, u_buf, start_buf, h_carry = scratch
    ts, width = a_buf.shape
    d = hp_buf.shape[1]
    heads, hd, _ = wax_ref.shape
    halo = CONV_HALO * SUBLANES

    gs = g * (1.0 + scale)
    for part in _splits(ts, 4):
        for r0 in range(part.start, part.stop, 2 * SUBLANES):
            rows = slice(r0, r0 + 2 * SUBLANES)
            xb = xp[rows, :]
            hp_buf[rows, :] = ((xb * _rms_scale(xb)) * gs + shift).astype(BF16)
        yield 164, (hp_buf,)

    for cols in _splits(width, 2):
        n = cols.stop - cols.start
        row = lax.broadcasted_iota(jnp.int32, (SUBLANES, n), 0)
        prev_tail = xbuf[ts:ts + halo, cols]
        xr = jnp.dot(hp_buf[...], win_ref[:, cols], preferred_element_type=F32) + bin_ref[:, cols]
        xbuf[halo:halo + ts, cols] = xr
        for p in range(CONV_HALO):
            blk = slice(p * SUBLANES, (p + 1) * SUBLANES)
            cur = xr[ts - halo + p * SUBLANES:ts - halo + (p + 1) * SUBLANES, :]
            xbuf[blk, cols] = _shift_segments(prev_tail[blk, :], cur, row)
        yield 0, (xbuf,)
    for cols in _splits(width, 4):
        part = cb_ref[:, cols]
        for k in range(CONV_WIDTH):
            part = part + xbuf[k * SUBLANES:k * SUBLANES + ts, cols] * cw_ref[k:k + 1, cols]
        xc_buf[:, cols] = part
        yield 256, (xc_buf,)

    log_sig = LRU_C * jax.nn.log_sigmoid(lam_ref[...])

    def gate_matmul(hh):
        park = slice((hh % 2) * 2 * hd, (hh % 2 + 1) * 2 * hd)
        mm_buf[:, park] = jnp.dot(xc_buf[:, hh * hd:(hh + 1) * hd].astype(BF16), wax_ref[hh],
                                  preferred_element_type=F32) + bax_ref(hh)

    def gate_math(hh, sub, rows):
        base = (hh % 2) * 2 * hd
        cols = slice(hh * hd + sub.start, hh * hd + sub.stop)
        gate_r = _sigmoid_of_twice(mm_buf[rows, base + sub.start:base + sub.stop])
        gate_i = _sigmoid_of_twice(mm_buf[rows, base + hd + sub.start:base + hd + sub.stop])
        log_a = gate_r * log_sig[:, cols]
        a_buf[rows, cols] = jnp.exp(log_a)
        t = jnp.tanh(log_a)
        q = -2.0 * t
        mult = jnp.where(q > 0.0, q * lax.rsqrt(q * (1.0 - t)), 0.0)
        u_buf[rows, cols] = mult * (gate_i * xc_buf[rows, cols])

    for first in range(0, heads, 2):
        for hh in range(first, min(first + 2, heads)):
            gate_matmul(hh)
            yield 0, (mm_buf,)
        for hh in range(first, min(first + 2, heads)):
            for sub in _splits(hd, hd // LANES):
                for rows in _splits(ts, 2):
                    gate_math(hh, sub, rows)
                    yield 240, (a_buf, u_buf)

    for cols in _splits(width, 2):
        n = cols.stop - cols.start
        row = lax.broadcasted_iota(jnp.int32, (SUBLANES, n), 0)
        seg_u = jnp.zeros((SUBLANES, n), F32)
        seg_a = jnp.ones((SUBLANES, n), F32)
        for j in range(SEG_LEN):
            blk = slice(j * SUBLANES, (j + 1) * SUBLANES)
            a = a_buf[blk, cols]
            seg_u = a * seg_u + u_buf[blk, cols]
            seg_a = a * seg_a
            u_buf[blk, cols] = seg_u
            a_buf[blk, cols] = seg_a
        k = 1
        while k < SUBLANES:
            keep = row >= k
            a_prev = jnp.where(keep, pltpu.roll(seg_a, k, 0), 1.0)
            u_prev = jnp.where(keep, pltpu.roll(seg_u, k, 0), 0.0)
            seg_u = seg_u + seg_a * u_prev
            seg_a = seg_a * a_prev
            k *= 2
        state_in = h_carry[0:1, cols]
        seg_end = seg_a * state_in + seg_u
        h_carry[0:1, cols] = seg_end[SUBLANES - 1:SUBLANES, :]
        start_buf[:, cols] = jnp.where(row == 0, state_in, pltpu.roll(seg_end, 1, 0))
        yield 200, (a_buf, u_buf)

    for cols in _splits(width, 2):
        mm_buf[:, cols] = (jnp.dot(hp_buf[...], wy_ref[:, cols], preferred_element_type=F32)
                           + by_ref[:, cols])
        yield 0, (mm_buf,)
    for cols in _splits(width, width // LANES):
        n = cols.stop - cols.start
        hs = (u_buf[:, cols].reshape(SEG_LEN, SUBLANES, n)
              + a_buf[:, cols].reshape(SEG_LEN, SUBLANES, n) * start_buf[:, cols][None]
              ).reshape(ts, n)
        hp_buf[:, cols] = (hs * _gelu_tanh(mm_buf[:, cols])).astype(BF16)
        yield 180, (hp_buf,)

    for cols in _splits(d, 2):
        y = jnp.dot(hp_buf[...], wout_ref[:, cols], preferred_element_type=F32) + bout_ref[:, cols]
        x1_buf[:, cols] = xp[:, cols] + gate[:, cols] * y
        yield 64, (x1_buf,)


def _lru_costs(heads, hd, width):
    per_pair = [0, 0] + [240] * (2 * 2 * (hd // LANES))
    return ([164] * 4 + [0] * 2 + [256] * 4 + per_pair * (heads // 2)
            + [200] * 2 + [0] * 2 + [180] * (width // LANES) + [64] * 2)


def _pool_reset(scratch):
    (hbuf,) = scratch
    ts = hbuf.shape[0] - POOL_HALO * SUBLANES
    hbuf[ts:, :] = jnp.zeros((POOL_HALO * SUBLANES, hbuf.shape[1]), F32)


def _pool_phases(xp, g, shift, scale, gate, x1_buf, weights, scratch, tile_start):
    pw_ref, ps_ref = weights
    (hbuf,) = scratch
    halo = POOL_HALO * SUBLANES
    ts = hbuf.shape[0] - halo
    d = hbuf.shape[1]
    groups, gd, _ = pw_ref.shape

    gs = g * (1.0 + scale)
    row = lax.broadcasted_iota(jnp.int32, (SUBLANES, d), 0)
    parts = _splits(ts, 4)
    assert parts[-1].start == ts - halo
    for part in (parts[-1],) + parts[:-1]:
        for r0 in range(part.start, part.stop, 2 * SUBLANES):
            rows = slice(halo + r0, halo + r0 + 2 * SUBLANES)
            xb = xp[r0:r0 + 2 * SUBLANES, :]
            hb = (xb * _rms_scale(xb)) * gs + shift
            if r0 >= ts - halo:
                old = hbuf[rows, :]
                for i in range(2):
                    blk = slice(i * SUBLANES, (i + 1) * SUBLANES)
                    p0 = r0 - (ts - halo) + i * SUBLANES
                    hbuf[p0:p0 + SUBLANES, :] = _shift_segments(old[blk, :], hb[blk, :], row)
            hbuf[rows, :] = hb
        yield 190, (hbuf,)

    m = lax.broadcasted_iota(jnp.int32, (ts, 1), 0)
    pos = tile_start + (m % SUBLANES) * SEG_LEN + m // SUBLANES + 1
    for gi, win in enumerate(POOL_WINDOWS):
        cols = slice(gi * gd, (gi + 1) * gd)
        total = hbuf[(POOL_HALO - (win - 1)) * SUBLANES:, cols]
        step = 1
        while step < win:
            total = total[step * SUBLANES:, :] + total[:-step * SUBLANES, :]
            step *= 2
        inv_count = 1.0 / jnp.minimum(pos, win).astype(F32)
        pooled = total * inv_count - hbuf[halo:, cols]
        mixed = jnp.dot(pooled.astype(BF16), pw_ref[gi], preferred_element_type=F32)
        x1_buf[:, cols] = xp[:, cols] + gate[:, cols] * (mixed * ps_ref[:, cols])
        yield 280, (x1_buf,)


def _pool_costs(groups):
    return [190] * 4 + [280] * groups


PERMUTE_IN_COSTS = [0] * 4 + [70] * 2
HAND_OVER_COSTS = [190, 190, 190, 190]


def _layer_kernel(*refs, kind, first, final, tiles_per_seq, n_mixer_weights,
                  n_mixer_scratch):
    x_ref, modm_ref, modf_ref, vec_ref = refs[:4]
    mixer_mats = refs[4:4 + n_mixer_weights]
    w1_ref, w2_ref, o_ref = refs[4 + n_mixer_weights:7 + n_mixer_weights]
    rest = refs[7 + n_mixer_weights:]
    gm_ref, gf_ref = _Rows(vec_ref, 0), _Rows(vec_ref, 1)
    if kind == "lru":
        wy_ref, win_ref, wax_ref, wout_ref = mixer_mats
        heads, hd, _ = wax_ref.shape
        per_row = vec_ref.shape[1] // (2 * hd)
        mixer_weights = (
            wy_ref, _Rows(vec_ref, 2), win_ref, _Rows(vec_ref, 3), _Rows(vec_ref, 4, CONV_WIDTH),
            _Rows(vec_ref, 8), wax_ref,
            lambda hh: vec_ref[11 + hh // per_row:12 + hh // per_row,
                               (hh % per_row) * 2 * hd:(hh % per_row + 1) * 2 * hd],
            _Rows(vec_ref, 9), wout_ref, _Rows(vec_ref, 10))
        fg_ref = _Rows(vec_ref, 13)
    else:
        mixer_weights = (mixer_mats[0], _Rows(vec_ref, 2))
        fg_ref = _Rows(vec_ref, 3)
    x1_buf, h1_buf = rest[:2]
    rest = rest[2:]
    slab = xp_buf = None
    if first or final:
        slab, rest = rest[0], rest[1:]
    if first:
        xp_buf, rest = rest[0], rest[1:]
    mixer_scratch = rest[:n_mixer_scratch]
    ts, d = x_ref.shape
    g = pl.program_id(0)

    def run_if(cond, body, trips=1):
        lax.fori_loop(0, jnp.where(cond, trips, 0), lambda i, c: (body(i), c)[1], 0)

    def clear_rows(i):
        pair = 2 * SUBLANES
        rows = pl.ds(pl.multiple_of(i * pair, pair), pair)
        x1_buf[rows, :] = jnp.zeros((pair, d), F32)
        h1_buf[rows, :] = jnp.zeros((pair, d), BF16)

    run_if(g == 0, clear_rows, trips=ts // (2 * SUBLANES))
    run_if(g % tiles_per_seq == 0,
           lambda i: (_lru_reset if kind == "lru" else _pool_reset)(mixer_scratch))

    xp = xp_buf if first else x_ref
    norm_args = (xp, gm_ref[...], modm_ref[0:1, :], modm_ref[1:2, :], modm_ref[2:3, :], x1_buf)
    if kind == "lru":
        heads, hd, _ = mixer_weights[6].shape
        assert heads % 2 == 0 and 4 * hd <= d
        mixer = _lru_phases(*norm_args, mixer_weights, mixer_scratch)
        costs = _lru_costs(heads, hd, d)
    else:
        mixer = _pool_phases(*norm_args, mixer_weights, mixer_scratch,
                             (g % tiles_per_seq) * ts)
        costs = _pool_costs(mixer_weights[0].shape[0])
    costs = (PERMUTE_IN_COSTS if first else []) + costs + HAND_OVER_COSTS

    def all_phases():
        if first:
            yield from _permute_in(x_ref, slab, xp_buf)
        yield from mixer
        gs = gf_ref[...] * (1.0 + modm_ref[4:5, :])
        for part in _splits(ts, 4):
            for r0 in range(part.start, part.stop, 2 * SUBLANES):
                rows = slice(r0, r0 + 2 * SUBLANES)
                x1 = x1_buf[rows, :]
                h1_buf[rows, :] = (
                    (x1 * _rms_scale(x1)) * gs + modm_ref[3:4, :]).astype(BF16)
            yield 190, (h1_buf,)

    phases = all_phases()
    total_cost = float(sum(costs) - sum(HAND_OVER_COSTS))
    spent = 0.0
    n_emitted = 0
    written = []

    def fill(fraction):
        nonlocal spent, n_emitted
        while n_emitted < len(costs) and spent < fraction * total_cost:
            cost, refs = next(phases)
            assert cost == costs[n_emitted], (n_emitted, cost, costs[n_emitted])
            spent += cost
            n_emitted += 1
            written.extend(r for r in refs if all(r is not w for w in written))

    def anchor(target):
        pair = 2 * SUBLANES
        rows = pl.ds(pl.multiple_of(lax.shift_right_logical(g, 30) * pair, pair), pair)
        bits = None
        for ref in written:
            tok = ref[0, rows, :] if len(ref.shape) == 3 else ref[rows, 0:LANES]
            b = pltpu.bitcast(tok, jnp.uint32)
            if b.shape[0] == pair:
                b = b[:SUBLANES] | b[SUBLANES:]
            bits = b if bits is None else bits | b
        del written[:]
        if bits is not None:
            zero = pltpu.bitcast(lax.shift_right_logical(bits, jnp.uint32(32)), BF16)
            target[0:pair, 0:LANES] = target[0:pair, 0:LANES] + zero

    d_ff = w1_ref.shape[1]
    n_chunks = d_ff // FF_CHUNK
    n_gaps = 2 * n_chunks - 1
    acc = None
    for k in range(n_chunks):
        if k > 0:
            anchor(h1_buf)
        cols = slice(k * FF_CHUNK, (k + 1) * FF_CHUNK)
        u = jnp.maximum(jnp.dot(h1_buf[...], w1_ref[:, cols], preferred_element_type=F32), 0.0)
        u = (u * u).astype(BF16)
        if k == 0:
            o_ref[...] = x1_buf[...]
        fill((2 * k + 1) / n_gaps)
        p = jnp.dot(u, w2_ref[cols, :], preferred_element_type=F32)
        acc = p if acc is None else acc + p
        fill((2 * k + 2) / n_gaps if k + 1 < n_chunks else 2.0)
    assert n_emitted == len(costs) and next(phases, "done") == "done"
    out = o_ref[...] + modf_ref[5:6, :] * acc
    if final:
        out = (out * _rms_scale(out)) * fg_ref[...]
        _permute_out(out, slab, o_ref)
    else:
        o_ref[...] = out


def _layer(x, mod, layer, kind, vec, mixer_mats, j, w1, w2, first, final):
    n_tiles, ts, d = x.shape
    bsz = mod.shape[1]
    tiles_per_seq = n_tiles // bsz
    last = n_tiles - 1

    def resident(op, idx):
        nd = op.ndim - 1
        return pl.BlockSpec((None,) + op.shape[1:], lambda g: (idx,) + (0,) * nd,
                            pipeline_mode=pl.Buffered(1))

    def mix_tile(g):
        return jnp.minimum(g, last)

    def mlp_tile(g):
        return jnp.maximum(g - 1, 0)

    operands = [x, mod, mod, vec, *mixer_mats, w1, w2]
    in_specs = [
        pl.BlockSpec((None, ts, d), lambda g: (mix_tile(g), 0, 0)),
        pl.BlockSpec((None, None, N_MOD, d),
                     lambda g: (layer, mix_tile(g) // tiles_per_seq, 0, 0)),
        pl.BlockSpec((None, None, N_MOD, d),
                     lambda g: (layer, mlp_tile(g) // tiles_per_seq, 0, 0)),
        resident(vec, j),
        *[resident(op, j) for op in mixer_mats],
        resident(w1, layer), resident(w2, layer),
    ]

    scratch = [
        pltpu.VMEM((ts, d), F32),
        pltpu.VMEM((ts, d), BF16),
    ]
    if first or final:
        scratch.append(pltpu.VMEM((d // LANES, SUBLANES * SEG_PITCH, LANES), F32))
    if first:
        scratch.append(pltpu.VMEM((ts, d), F32))
    if kind == "lru":
        width = mixer_mats[0].shape[2]
        assert width == d
        mixer_scratch = [
            pltpu.VMEM((ts, d), BF16),
            pltpu.VMEM((ts + CONV_HALO * SUBLANES, width), F32),
            pltpu.VMEM((ts, width), F32),
            pltpu.VMEM((ts, width), F32),
            pltpu.VMEM((ts, width), F32),
            pltpu.VMEM((ts, width), F32),
            pltpu.VMEM((SUBLANES, width), F32),
            pltpu.VMEM((SUBLANES, width), F32),
        ]
    else:
        mixer_scratch = [pltpu.VMEM((ts + POOL_HALO * SUBLANES, d), F32)]

    return pl.pallas_call(
        functools.partial(_layer_kernel, kind=kind, first=first, final=final,
                          tiles_per_seq=tiles_per_seq, n_mixer_weights=len(mixer_mats),
                          n_mixer_scratch=len(mixer_scratch)),
        grid=(n_tiles + 1,),
        in_specs=in_specs,
        out_specs=pl.BlockSpec((None, ts, d), lambda g: (mlp_tile(g), 0, 0)),
        out_shape=jax.ShapeDtypeStruct(x.shape, F32),
        scratch_shapes=scratch + mixer_scratch,
        compiler_params=pltpu.CompilerParams(
            dimension_semantics=("arbitrary",), vmem_limit_bytes=VMEM_LIMIT),
        name=kind + "_layer" + ("_first" if first else "") + ("_final" if final else ""),
    )(*operands)


def kernel(x, c, w_mod, b_mod, norm_mix_g, norm_ffn_g, lru_w_y, lru_b_y, lru_w_in, lru_b_in,
           lru_conv_w, lru_conv_b, lru_w_a, lru_b_a, lru_w_x, lru_b_x, lru_lambda, lru_w_out,
           lru_b_out, pool_w, pool_scale, ffn_w1, ffn_w2, final_norm_g):
    depth = w_mod.shape[0]
    bsz, seq, d = x.shape
    assert seq % SEQ_TILE == 0 and ffn_w1.shape[2] % FF_CHUNK == 0
    assert w_mod.shape[2] % MOD_COLS == 0 and d % (4 * LANES) == 0
    mod = _modulation(c, w_mod, b_mod)

    n_a, heads, hd, _ = lru_w_a.shape
    n_b = pool_w.shape[0]
    w_ax = (0.5 * jnp.concatenate([lru_w_a, lru_w_x], axis=-1)).astype(BF16)
    b_ax = (0.5 * jnp.concatenate([lru_b_a, lru_b_x], axis=-1)).reshape(n_a, -1, d)
    lru_mats = (lru_w_y.astype(BF16), lru_w_in.astype(BF16), w_ax, lru_w_out.astype(BF16))
    pool_mats = (pool_w.astype(BF16),)
    w1, w2 = ffn_w1.astype(BF16), ffn_w2.astype(BF16)

    def rows(v):
        return v[:, None, :]

    def last_norm(n):
        return jnp.broadcast_to(final_norm_g, (n, 1, d))

    assert b_ax.shape[1] == 2
    vec_lru = jnp.concatenate(
        [rows(norm_mix_g[0::N_MIXERS]), rows(norm_ffn_g[0::N_MIXERS]), rows(lru_b_y),
         rows(lru_b_in), lru_conv_w, rows(lru_conv_b), rows(lru_lambda), rows(lru_b_out), b_ax,
         last_norm(n_a), jnp.zeros((n_a, 2, d), F32)], axis=1)
    vec_pool = jnp.concatenate(
        [rows(norm_mix_g[1::N_MIXERS]), rows(norm_ffn_g[1::N_MIXERS]), rows(pool_scale),
         last_norm(n_b)], axis=1)

    x = x.reshape(bsz * seq // SEQ_TILE, SEQ_TILE, d)
    for i in range(depth):
        j = i // N_MIXERS
        if i % N_MIXERS == 0:
            kind, vec, mats = "lru", vec_lru, lru_mats
        else:
            kind, vec, mats = "pool", vec_pool, pool_mats
        x = _layer(x, mod, i, kind, vec, mats, j, w1, w2, i == 0, i == depth - 1)
    return x.reshape(bsz, seq, d)
```

```python
import functools
import math

import jax
import jax.numpy as jnp
from jax import lax
from jax.experimental import pallas as pl
from jax.experimental.pallas import tpu as pltpu

F32 = jnp.float32
BF16 = jnp.bfloat16

N_MIXERS = 2
N_MOD = 6
CONV_WIDTH = 4
LRU_C = 8.0
POOL_WINDOWS = (2, 4, 8, 16)
EPS = 1e-6

SUBLANES = 8
LANES = 128
SEQ_TILE = 512
SEG_LEN = SEQ_TILE // SUBLANES
SEG_PITCH = SEG_LEN + SUBLANES
FF_CHUNK = 1024
MOD_COLS = 1536
CONV_HALO = CONV_WIDTH - 1
POOL_HALO = max(POOL_WINDOWS)
VMEM_LIMIT = 58 * 1024 * 1024

GELU_C1 = math.sqrt(2.0 / math.pi)
GELU_C2 = GELU_C1 * 0.044715


def _rms_scale(x):
    return lax.rsqrt(jnp.mean(x * x, axis=-1, keepdims=True) + EPS)


def _splits(n, parts):
    step = n // parts
    return tuple(slice(i * step, (i + 1) * step) for i in range(parts))


class _Rows:
    def __init__(self, ref, first, count=1):
        self.ref, self.first, self.count = ref, first, count

    def __getitem__(self, idx):
        if idx is Ellipsis:
            return self.ref[self.first:self.first + self.count, :]
        rows, cols = idx
        start = self.first + (rows.start or 0)
        stop = self.first + (self.count if rows.stop is None else rows.stop)
        return self.ref[start:stop, cols]


def _mod_kernel(c_ref, w_ref, b_ref, o_ref):
    c = c_ref[...]
    cond = c * jax.nn.sigmoid(c)
    o_ref[...] = jnp.dot(cond.astype(BF16), w_ref[...].astype(BF16),
                         preferred_element_type=F32) + b_ref[...]


def _modulation(c, w_mod, b_mod):
    depth, d, e = w_mod.shape
    bsz = c.shape[0]
    rows = -(-bsz // SUBLANES) * SUBLANES
    c_pad = jnp.zeros((rows, d), F32).at[:bsz].set(c)
    out = pl.pallas_call(
        _mod_kernel,
        grid=(depth, e // MOD_COLS),
        in_specs=[
            pl.BlockSpec((rows, d), lambda i, j: (0, 0)),
            pl.BlockSpec((None, d, MOD_COLS), lambda i, j: (i, 0, j)),
            pl.BlockSpec((None, 1, MOD_COLS), lambda i, j: (i, 0, j)),
        ],
        out_specs=pl.BlockSpec((None, rows, MOD_COLS), lambda i, j: (i, 0, j)),
        out_shape=jax.ShapeDtypeStruct((depth, rows, e), F32),
        compiler_params=pltpu.CompilerParams(
            dimension_semantics=("arbitrary", "arbitrary"), vmem_limit_bytes=VMEM_LIMIT),
        name="adaln_mod",
    )(c_pad, w_mod, b_mod.reshape(depth, 1, e))
    return out[:, :bsz].reshape(depth, bsz, N_MOD, d)


def _permute_in(x_ref, slab, xp_buf):
    d = x_ref.shape[1]
    for cols in _splits(d, 4):
        for c in range(cols.start // LANES, cols.stop // LANES):
            for r in range(SUBLANES):
                slab[c, r * SEG_PITCH:r * SEG_PITCH + SEG_LEN, :] = (
                    x_ref[r * SEG_LEN:(r + 1) * SEG_LEN, c * LANES:(c + 1) * LANES])
        yield 0, (slab,)
    for js in _splits(SEG_LEN, 2):
        for j in range(js.start, js.stop):
            for c in range(d // LANES):
                xp_buf[j * SUBLANES:(j + 1) * SUBLANES, c * LANES:(c + 1) * LANES] = (
                    slab[c, pl.ds(j, SUBLANES, stride=SEG_PITCH), :])
        yield 70, (xp_buf,)


def _permute_out(y, slab, o_ref):
    d = y.shape[1]
    for c in range(d // LANES):
        slab[c, 0:y.shape[0], :] = y[:, c * LANES:(c + 1) * LANES]
    for q in range(SEG_LEN // SUBLANES):
        for r in range(SUBLANES):
            src = q * SUBLANES * SUBLANES + r
            dst = slice(r * SEG_LEN + q * SUBLANES, r * SEG_LEN + (q + 1) * SUBLANES)
            for c in range(d // LANES):
                o_ref[dst, c * LANES:(c + 1) * LANES] = (
                    slab[c, pl.ds(src, SUBLANES, stride=SUBLANES), :])


def _shift_segments(prev_block, cur_block, row):
    return jnp.where(row == 0, pltpu.roll(prev_block, 1, 0), pltpu.roll(cur_block, 1, 0))


def _sigmoid_of_twice(v):
    return 0.5 * jnp.tanh(v) + 0.5


def _gelu_tanh(v):
    return (0.5 * v) * (1.0 + jnp.tanh(v * (GELU_C1 + GELU_C2 * (v * v))))


def _lru_reset(scratch):
    hp_buf, xbuf, xc_buf, mm_buf, a_buf, u_buf, start_buf, h_carry = scratch
    ts, width = a_buf.shape
    xbuf[ts:ts + CONV_HALO * SUBLANES, :] = jnp.zeros((CONV_HALO * SUBLANES, width), F32)
    h_carry[...] = jnp.zeros_like(h_carry)


def _lru_phases(xp, g, shift, scale, gate, x1_buf, weights, scratch):
    wy_ref, by_ref, win_ref, bin_ref, cw_ref, cb_ref, wax_ref, bax_ref, lam_ref, wout_ref, \
        bout_ref = weights
    hp_buf, xbuf, xc_buf, mm_buf, a_buf, u_buf, start_buf, h_carry = scratch
    ts, width = a_buf.shape
    d = hp_buf.shape[1]
    heads, hd, _ = wax_ref.shape
    halo = CONV_HALO * SUBLANES

    gs = g * (1.0 + scale)
    for part in _splits(ts, 4):
        for r0 in range(part.start, part.stop, 2 * SUBLANES):
            rows = slice(r0, r0 + 2 * SUBLANES)
            xb = xp[rows, :]
            hp_buf[rows, :] = ((xb * _rms_scale(xb)) * gs + shift).astype(BF16)
        yield 164, (hp_buf,)

    for cols in _splits(width, 2):
        n = cols.stop - cols.start
        row = lax.broadcasted_iota(jnp.int32, (SUBLANES, n), 0)
        prev_tail = xbuf[ts:ts + halo, cols]
        xr = jnp.dot(hp_buf[...], win_ref[:, cols], preferred_element_type=F32) + bin_ref[:, cols]
        xbuf[halo:halo + ts, cols] = xr
        for p in range(CONV_HALO):
            blk = slice(p * SUBLANES, (p + 1) * SUBLANES)
            cur = xr[ts - halo + p * SUBLANES:ts - halo + (p + 1) * SUBLANES, :]
            xbuf[blk, cols] = _shift_segments(prev_tail[blk, :], cur, row)
        yield 0, (xbuf,)
    for cols in _splits(width, 4):
        part = cb_ref[:, cols]
        for k in range(CONV_WIDTH):
            part = part + xbuf[k * SUBLANES:k * SUBLANES + ts, cols] * cw_ref[k:k + 1, cols]
        xc_buf[:, cols] = part
        yield 256, (xc_buf,)

    log_sig = LRU_C * jax.nn.log_sigmoid(lam_ref[...])

    def gate_matmul(hh):
        park = slice((hh % 2) * 2 * hd, (hh % 2 + 1) * 2 * hd)
        mm_buf[:, park] = jnp.dot(xc_buf[:, hh * hd:(hh + 1) * hd].astype(BF16), wax_ref[hh],
                                  preferred_element_type=F32) + bax_ref(hh)

    def gate_math(hh, sub, rows):
        base = (hh % 2) * 2 * hd
        cols = slice(hh * hd + sub.start, hh * hd + sub.stop)
        gate_r = _sigmoid_of_twice(mm_buf[rows, base + sub.start:base + sub.stop])
        gate_i = _sigmoid_of_twice(mm_buf[rows, base + hd + sub.start:base + hd + sub.stop])
        log_a = gate_r * log_sig[:, cols]
        a_buf[rows, cols] = jnp.exp(log_a)
        t = jnp.tanh(log_a)
        q = -2.0 * t
        mult = jnp.where(q > 0.0, q * lax.rsqrt(q * (1.0 - t)), 0.0)
        u_buf[rows, cols] = mult * (gate_i * xc_buf[rows, cols])

    for first in range(0, heads, 2):
        for hh in range(first, min(first + 2, heads)):
            gate_matmul(hh)
            yield 0, (mm_buf,)
        for hh in range(first, min(first + 2, heads)):
            for sub in _splits(hd, hd // LANES):
                for rows in _splits(ts, 2):
                    gate_math(hh, sub, rows)
                    yield 240, (a_buf, u_buf)

    for cols in _splits(width, 2):
        n = cols.stop - cols.start
        row = lax.broadcasted_iota(jnp.int32, (SUBLANES, n), 0)
        seg_u = jnp.zeros((SUBLANES, n), F32)
        seg_a = jnp.ones((SUBLANES, n), F32)
        for j in range(SEG_LEN):
            blk = slice(j * SUBLANES, (j + 1) * SUBLANES)
            a = a_buf[blk, cols]
            seg_u = a * seg_u + u_buf[blk, cols]
            seg_a = a * seg_a
            u_buf[blk, cols] = seg_u
            a_buf[blk, cols] = seg_a
        k = 1
        while k < SUBLANES:
            keep = row >= k
            a_prev = jnp.where(keep, pltpu.roll(seg_a, k, 0), 1.0)
            u_prev = jnp.where(keep, pltpu.roll(seg_u, k, 0), 0.0)
            seg_u = seg_u + seg_a * u_prev
            seg_a = seg_a * a_prev
            k *= 2
        state_in = h_carry[0:1, cols]
        seg_end = seg_a * state_in + seg_u
        h_carry[0:1, cols] = seg_end[SUBLANES - 1:SUBLANES, :]
        start_buf[:, cols] = jnp.where(row == 0, state_in, pltpu.roll(seg_end, 1, 0))
        yield 200, (a_buf, u_buf)

    for cols in _splits(width, 2):
        mm_buf[:, cols] = (jnp.dot(hp_buf[...], wy_ref[:, cols], preferred_element_type=F32)
                           + by_ref[:, cols])
        yield 0, (mm_buf,)
    for cols in _splits(width, width // LANES):
        n = cols.stop - cols.start
        hs = (u_buf[:, cols].reshape(SEG_LEN, SUBLANES, n)
              + a_buf[:, cols].reshape(SEG_LEN, SUBLANES, n) * start_buf[:, cols][None]
              ).reshape(ts, n)
        hp_buf[:, cols] = (hs * _gelu_tanh(mm_buf[:, cols])).astype(BF16)
        yield 180, (hp_buf,)

    for cols in _splits(d, 2):
        y = jnp.dot(hp_buf[...], wout_ref[:, cols], preferred_element_type=F32) + bout_ref[:, cols]
        x1_buf[:, cols] = xp[:, cols] + gate[:, cols] * y
        yield 64, (x1_buf,)


def _lru_costs(heads, hd, width):
    per_pair = [0, 0] + [240] * (2 * 2 * (hd // LANES))
    return ([164] * 4 + [0] * 2 + [256] * 4 + per_pair * (heads // 2)
            + [200] * 2 + [0] * 2 + [180] * (width // LANES) + [64] * 2)


def _pool_reset(scratch):
    (hbuf,) = scratch
    ts = hbuf.shape[0] - POOL_HALO * SUBLANES
    hbuf[ts:, :] = jnp.zeros((POOL_HALO * SUBLANES, hbuf.shape[1]), F32)


def _pool_phases(xp, g, shift, scale, gate, x1_buf, weights, scratch, tile_start):
    pw_ref, ps_ref = weights
    (hbuf,) = scratch
    halo = POOL_HALO * SUBLANES
    ts = hbuf.shape[0] - halo
    d = hbuf.shape[1]
    groups, gd, _ = pw_ref.shape

    gs = g * (1.0 + scale)
    row = lax.broadcasted_iota(jnp.int32, (SUBLANES, d), 0)
    parts = _splits(ts, 4)
    assert parts[-1].start == ts - halo
    for part in (parts[-1],) + parts[:-1]:
        for r0 in range(part.start, part.stop, 2 * SUBLANES):
            rows = slice(halo + r0, halo + r0 + 2 * SUBLANES)
            xb = xp[r0:r0 + 2 * SUBLANES, :]
            hb = (xb * _rms_scale(xb)) * gs + shift
            if r0 >= ts - halo:
                old = hbuf[rows, :]
                for i in range(2):
                    blk = slice(i * SUBLANES, (i + 1) * SUBLANES)
                    p0 = r0 - (ts - halo) + i * SUBLANES
                    hbuf[p0:p0 + SUBLANES, :] = _shift_segments(old[blk, :], hb[blk, :], row)
            hbuf[rows, :] = hb
        yield 190, (hbuf,)

    m = lax.broadcasted_iota(jnp.int32, (ts, 1), 0)
    pos = tile_start + (m % SUBLANES) * SEG_LEN + m // SUBLANES + 1
    for gi, win in enumerate(POOL_WINDOWS):
        cols = slice(gi * gd, (gi + 1) * gd)
        total = hbuf[(POOL_HALO - (win - 1)) * SUBLANES:, cols]
        step = 1
        while step < win:
            total = total[step * SUBLANES:, :] + total[:-step * SUBLANES, :]
            step *= 2
        inv_count = 1.0 / jnp.minimum(pos, win).astype(F32)
        pooled = total * inv_count - hbuf[halo:, cols]
        mixed = jnp.dot(pooled.astype(BF16), pw_ref[gi], preferred_element_type=F32)
        x1_buf[:, cols] = xp[:, cols] + gate[:, cols] * (mixed * ps_ref[:, cols])
        yield 280, (x1_buf,)


def _pool_costs(groups):
    return [190] * 4 + [280] * groups


PERMUTE_IN_COSTS = [0] * 4 + [70] * 2
HAND_OVER_COSTS = [190, 190, 190, 190]


def _layer_kernel(*refs, kind, first, final, tiles_per_seq, n_mixer_weights,
                  n_mixer_scratch):
    x_ref, modm_ref, modf_ref, vec_ref = refs[:4]
    mixer_mats = refs[4:4 + n_mixer_weights]
    w1_ref, w2_ref, o_ref = refs[4 + n_mixer_weights:7 + n_mixer_weights]
    rest = refs[7 + n_mixer_weights:]
    gm_ref, gf_ref = _Rows(vec_ref, 0), _Rows(vec_ref, 1)
    if kind == "lru":
        wy_ref, win_ref, wax_ref, wout_ref = mixer_mats
        heads, hd, _ = wax_ref.shape
        per_row = vec_ref.shape[1] // (2 * hd)
        mixer_weights = (
            wy_ref, _Rows(vec_ref, 2), win_ref, _Rows(vec_ref, 3), _Rows(vec_ref, 4, CONV_WIDTH),
            _Rows(vec_ref, 8), wax_ref,
            lambda hh: vec_ref[11 + hh // per_row:12 + hh // per_row,
                               (hh % per_row) * 2 * hd:(hh % per_row + 1) * 2 * hd],
            _Rows(vec_ref, 9), wout_ref, _Rows(vec_ref, 10))
        fg_ref = _Rows(vec_ref, 13)
    else:
        mixer_weights = (mixer_mats[0], _Rows(vec_ref, 2))
        fg_ref = _Rows(vec_ref, 3)
    x1_slots, h1_buf = rest[:2]
    rest = rest[2:]
    slab = xp_buf = None
    if first or final:
        slab, rest = rest[0], rest[1:]
    if first:
        xp_buf, rest = rest[0], rest[1:]
    mixer_scratch = rest[:n_mixer_scratch]
    ts, d = x_ref.shape
    g = pl.program_id(0)

    def run_if(cond, body, trips=1):
        lax.fori_loop(0, jnp.where(cond, trips, 0), lambda i, c: (body(i), c)[1], 0)

    def clear_rows(i):
        pair = 2 * SUBLANES
        rows = pl.ds(pl.multiple_of(i * pair, pair), pair)
        for slot in range(2):
            x1_slots[slot, rows, :] = jnp.zeros((pair, d), F32)
        h1_buf[rows, :] = jnp.zeros((pair, d), BF16)

    run_if(g == 0, clear_rows, trips=ts // (2 * SUBLANES))
    run_if(g % tiles_per_seq == 0,
           lambda i: (_lru_reset if kind == "lru" else _pool_reset)(mixer_scratch))

    x1_buf = x1_slots.at[g % 2]
    x1_prev = x1_slots.at[(g + 1) % 2]

    xp = xp_buf if first else x_ref
    norm_args = (xp, gm_ref[...], modm_ref[0:1, :], modm_ref[1:2, :], modm_ref[2:3, :], x1_buf)
    if kind == "lru":
        heads, hd, _ = mixer_weights[6].shape
        assert heads % 2 == 0 and 4 * hd <= d
        mixer = _lru_phases(*norm_args, mixer_weights, mixer_scratch)
        costs = _lru_costs(heads, hd, d)
    else:
        mixer = _pool_phases(*norm_args, mixer_weights, mixer_scratch,
                             (g % tiles_per_seq) * ts)
        costs = _pool_costs(mixer_weights[0].shape[0])
    costs = (PERMUTE_IN_COSTS if first else []) + costs + HAND_OVER_COSTS

    def all_phases():
        if first:
            yield from _permute_in(x_ref, slab, xp_buf)
        yield from mixer
        gs = gf_ref[...] * (1.0 + modm_ref[4:5, :])
        for part in _splits(ts, 4):
            for r0 in range(part.start, part.stop, 2 * SUBLANES):
                rows = slice(r0, r0 + 2 * SUBLANES)
                x1 = x1_buf[rows, :]
                h1_buf[rows, :] = (
                    (x1 * _rms_scale(x1)) * gs + modm_ref[3:4, :]).astype(BF16)
            yield 190, (h1_buf,)

    phases = all_phases()
    total_cost = float(sum(costs) - sum(HAND_OVER_COSTS))
    spent = 0.0
    n_emitted = 0
    written = []

    def fill(fraction):
        nonlocal spent, n_emitted
        while n_emitted < len(costs) and spent < fraction * total_cost:
            cost, refs = next(phases)
            assert cost == costs[n_emitted], (n_emitted, cost, costs[n_emitted])
            spent += cost
            n_emitted += 1
            written.extend(r for r in refs if all(r is not w for w in written))

    def anchor(target):
        pair = 2 * SUBLANES
        rows = pl.ds(pl.multiple_of(lax.shift_right_logical(g, 30) * pair, pair), pair)
        bits = None
        for ref in written:
            tok = ref[0, rows, :] if len(ref.shape) == 3 else ref[rows, 0:LANES]
            b = pltpu.bitcast(tok, jnp.uint32)
            if b.shape[0] == pair:
                b = b[:SUBLANES] | b[SUBLANES:]
            bits = b if bits is None else bits | b
        del written[:]
        if bits is not None:
            zero = pltpu.bitcast(lax.shift_right_logical(bits, jnp.uint32(32)), BF16)
            target[0:pair, 0:LANES] = target[0:pair, 0:LANES] + zero

    d_ff = w1_ref.shape[1]
    n_chunks = d_ff // FF_CHUNK
    n_gaps = 2 * n_chunks - 1
    acc = None
    for k in range(n_chunks):
        if k > 0:
            anchor(h1_buf)
        cols = slice(k * FF_CHUNK, (k + 1) * FF_CHUNK)
        u = jnp.maximum(jnp.dot(h1_buf[...], w1_ref[:, cols], preferred_element_type=F32), 0.0)
        u = (u * u).astype(BF16)
        fill((2 * k + 1) / n_gaps)
        p = jnp.dot(u, w2_ref[cols, :], preferred_element_type=F32)
        acc = p if acc is None else acc + p
        fill((2 * k + 2) / n_gaps if k + 1 < n_chunks else 2.0)
    assert n_emitted == len(costs) and next(phases, "done") == "done"
    out = x1_prev[...] + modf_ref[5:6, :] * acc
    if final:
        out = (out * _rms_scale(out)) * fg_ref[...]
        _permute_out(out, slab, o_ref)
    else:
        o_ref[...] = out


def _layer(x, mod, layer, kind, vec, mixer_mats, j, w1, w2, first, final):
    n_tiles, ts, d = x.shape
    bsz = mod.shape[1]
    tiles_per_seq = n_tiles // bsz
    last = n_tiles - 1

    def resident(op, idx):
        nd = op.ndim - 1
        return pl.BlockSpec((None,) + op.shape[1:], lambda g: (idx,) + (0,) * nd,
                            pipeline_mode=pl.Buffered(1))

    def mix_tile(g):
        return jnp.minimum(g, last)

    def mlp_tile(g):
        return jnp.maximum(g - 1, 0)

    operands = [x, mod, mod, vec, *mixer_mats, w1, w2]
    in_specs = [
        pl.BlockSpec((None, ts, d), lambda g: (mix_tile(g), 0, 0)),
        pl.BlockSpec((None, None, N_MOD, d),
                     lambda g: (layer, mix_tile(g) // tiles_per_seq, 0, 0)),
        pl.BlockSpec((None, None, N_MOD, d),
                     lambda g: (layer, mlp_tile(g) // tiles_per_seq, 0, 0)),
        resident(vec, j),
        *[resident(op, j) for op in mixer_mats],
        resident(w1, layer), resident(w2, layer),
    ]

    scratch = [
        pltpu.VMEM((2, ts, d), F32),
        pltpu.VMEM((ts, d), BF16),
    ]
    if first or final:
        scratch.append(pltpu.VMEM((d // LANES, SUBLANES * SEG_PITCH, LANES), F32))
    if first:
        scratch.append(pltpu.VMEM((ts, d), F32))
    if kind == "lru":
        width = mixer_mats[0].shape[2]
        assert width == d
        mixer_scratch = [
            pltpu.VMEM((ts, d), BF16),
            pltpu.VMEM((ts + CONV_HALO * SUBLANES, width), F32),
            pltpu.VMEM((ts, width), F32),
            pltpu.VMEM((ts, width), F32),
            pltpu.VMEM((ts, width), F32),
            pltpu.VMEM((ts, width), F32),
            pltpu.VMEM((SUBLANES, width), F32),
            pltpu.VMEM((SUBLANES, width), F32),
        ]
    else:
        mixer_scratch = [pltpu.VMEM((ts + POOL_HALO * SUBLANES, d), F32)]

    return pl.pallas_call(
        functools.partial(_layer_kernel, kind=kind, first=first, final=final,
                          tiles_per_seq=tiles_per_seq, n_mixer_weights=len(mixer_mats),
                          n_mixer_scratch=len(mixer_scratch)),
        grid=(n_tiles + 1,),
        in_specs=in_specs,
        out_specs=pl.BlockSpec((None, ts, d), lambda g: (mlp_tile(g), 0, 0)),
        out_shape=jax.ShapeDtypeStruct(x.shape, F32),
        scratch_shapes=scratch + mixer_scratch,
        compiler_params=pltpu.CompilerParams(
            dimension_semantics=("arbitrary",), vmem_limit_bytes=VMEM_LIMIT),
        name=kind + "_layer" + ("_first" if first else "") + ("_final" if final else ""),
    )(*operands)


def kernel(x, c, w_mod, b_mod, norm_mix_g, norm_ffn_g, lru_w_y, lru_b_y, lru_w_in, lru_b_in,
           lru_conv_w, lru_conv_b, lru_w_a, lru_b_a, lru_w_x, lru_b_x, lru_lambda, lru_w_out,
           lru_b_out, pool_w, pool_scale, ffn_w1, ffn_w2, final_norm_g):
    depth = w_mod.shape[0]
    bsz, seq, d = x.shape
    assert seq % SEQ_TILE == 0 and ffn_w1.shape[2] % FF_CHUNK == 0
    assert w_mod.shape[2] % MOD_COLS == 0 and d % (4 * LANES) == 0
    mod = _modulation(c, w_mod, b_mod)

    n_a, heads, hd, _ = lru_w_a.shape
    n_b = pool_w.shape[0]
    w_ax = (0.5 * jnp.concatenate([lru_w_a, lru_w_x], axis=-1)).astype(BF16)
    b_ax = (0.5 * jnp.concatenate([lru_b_a, lru_b_x], axis=-1)).reshape(n_a, -1, d)
    lru_mats = (lru_w_y.astype(BF16), lru_w_in.astype(BF16), w_ax, lru_w_out.astype(BF16))
    pool_mats = (pool_w.astype(BF16),)
    w1, w2 = ffn_w1.astype(BF16), ffn_w2.astype(BF16)

    def rows(v):
        return v[:, None, :]

    def last_norm(n):
        return jnp.broadcast_to(final_norm_g, (n, 1, d))

    assert b_ax.shape[1] == 2
    vec_lru = jnp.concatenate(
        [rows(norm_mix_g[0::N_MIXERS]), rows(norm_ffn_g[0::N_MIXERS]), rows(lru_b_y),
         rows(lru_b_in), lru_conv_w, rows(lru_conv_b), rows(lru_lambda), rows(lru_b_out), b_ax,
         last_norm(n_a), jnp.zeros((n_a, 2, d), F32)], axis=1)
    vec_pool = jnp.concatenate(
        [rows(norm_mix_g[1::N_MIXERS]), rows(norm_ffn_g[1::N_MIXERS]), rows(pool_scale),
         last_norm(n_b)], axis=1)

    x = x.reshape(bsz * seq // SEQ_TILE, SEQ_TILE, d)
    for i in range(depth):
        j = i // N_MIXERS
        if i % N_MIXERS == 0:
            kind, vec, mats = "lru", vec_lru, lru_mats
        else:
            kind, vec, mats = "pool", vec_pool, pool_mats
        x = _layer(x, mod, i, kind, vec, mats, j, w1, w2, i == 0, i == depth - 1)
    return x.reshape(bsz, seq, d)
```

```python
import functools
import math

import jax
import jax.numpy as jnp
from jax import lax
from jax.experimental import pallas as pl
from jax.experimental.pallas import tpu as pltpu

F32 = jnp.float32
BF16 = jnp.bfloat16

N_MIXERS = 2
N_MOD = 6
CONV_WIDTH = 4
LRU_C = 8.0
POOL_WINDOWS = (2, 4, 8, 16)
EPS = 1e-6

SUBLANES = 8
LANES = 128
SEQ_TILE = 512
SEG_LEN = SEQ_TILE // SUBLANES
SEG_PITCH = SEG_LEN + SUBLANES
FF_CHUNK = 1024
MOD_COLS = 1536
CONV_HALO = CONV_WIDTH - 1
POOL_HALO = max(POOL_WINDOWS)
VMEM_LIMIT = 58 * 1024 * 1024

GELU_C1 = math.sqrt(2.0 / math.pi)
GELU_C2 = GELU_C1 * 0.044715


def _rms_scale(x):
    return lax.rsqrt(jnp.mean(x * x, axis=-1, keepdims=True) + EPS)


def _splits(n, parts):
    step = n // parts
    return tuple(slice(i * step, (i + 1) * step) for i in range(parts))


class _Rows:
    def __init__(self, ref, first, count=1):
        self.ref, self.first, self.count = ref, first, count

    def __getitem__(self, idx):
        if idx is Ellipsis:
            return self.ref[self.first:self.first + self.count, :]
        rows, cols = idx
        start = self.first + (rows.start or 0)
        stop = self.first + (self.count if rows.stop is None else rows.stop)
        return self.ref[start:stop, cols]


def _mod_kernel(c_ref, w_ref, b_ref, o_ref):
    c = c_ref[...]
    cond = c * jax.nn.sigmoid(c)
    o_ref[...] = jnp.dot(cond.astype(BF16), w_ref[...].astype(BF16),
                         preferred_element_type=F32) + b_ref[...]


def _modulation(c, w_mod, b_mod):
    depth, d, e = w_mod.shape
    bsz = c.shape[0]
    rows = -(-bsz // SUBLANES) * SUBLANES
    c_pad = jnp.zeros((rows, d), F32).at[:bsz].set(c)
    out = pl.pallas_call(
        _mod_kernel,
        grid=(depth, e // MOD_COLS),
        in_specs=[
            pl.BlockSpec((rows, d), lambda i, j: (0, 0)),
            pl.BlockSpec((None, d, MOD_COLS), lambda i, j: (i, 0, j)),
            pl.BlockSpec((None, 1, MOD_COLS), lambda i, j: (i, 0, j)),
        ],
        out_specs=pl.BlockSpec((None, rows, MOD_COLS), lambda i, j: (i, 0, j)),
        out_shape=jax.ShapeDtypeStruct((depth, rows, e), F32),
        compiler_params=pltpu.CompilerParams(
            dimension_semantics=("arbitrary", "arbitrary"), vmem_limit_bytes=VMEM_LIMIT),
        name="adaln_mod",
    )(c_pad, w_mod, b_mod.reshape(depth, 1, e))
    return out[:, :bsz].reshape(depth, bsz, N_MOD, d)


def _permute_in(x_ref, slab, xp_buf):
    d = x_ref.shape[1]
    for cols in _splits(d, 4):
        for c in range(cols.start // LANES, cols.stop // LANES):
            for r in range(SUBLANES):
                slab[c, r * SEG_PITCH:r * SEG_PITCH + SEG_LEN, :] = (
                    x_ref[r * SEG_LEN:(r + 1) * SEG_LEN, c * LANES:(c + 1) * LANES])
        yield 0, (slab,)
    for js in _splits(SEG_LEN, 2):
        for j in range(js.start, js.stop):
            for c in range(d // LANES):
                xp_buf[j * SUBLANES:(j + 1) * SUBLANES, c * LANES:(c + 1) * LANES] = (
                    slab[c, pl.ds(j, SUBLANES, stride=SEG_PITCH), :])
        yield 70, (xp_buf,)


def _permute_out(y, slab, o_ref):
    d = y.shape[1]
    for c in range(d // LANES):
        slab[c, 0:y.shape[0], :] = y[:, c * LANES:(c + 1) * LANES]
    for q in range(SEG_LEN // SUBLANES):
        for r in range(SUBLANES):
            src = q * SUBLANES * SUBLANES + r
            dst = slice(r * SEG_LEN + q * SUBLANES, r * SEG_LEN + (q + 1) * SUBLANES)
            for c in range(d // LANES):
                o_ref[dst, c * LANES:(c + 1) * LANES] = (
                    slab[c, pl.ds(src, SUBLANES, stride=SUBLANES), :])


def _shift_segments(prev_block, cur_block, row):
    return jnp.where(row == 0, pltpu.roll(prev_block, 1, 0), pltpu.roll(cur_block, 1, 0))


def _sigmoid_of_twice(v):
    return 0.5 * jnp.tanh(v) + 0.5


def _gelu_tanh(v):
    return (0.5 * v) * (1.0 + jnp.tanh(v * (GELU_C1 + GELU_C2 * (v * v))))


def _lru_reset(scratch, b_in):
    hp_buf, xbuf, xc_buf, mm_buf, a_buf, u_buf, start_buf, h_carry = scratch
    ts, width = a_buf.shape
    xbuf[ts:ts + CONV_HALO * SUBLANES, :] = jnp.broadcast_to(
        -b_in[...], (CONV_HALO * SUBLANES, width))
    h_carry[...] = jnp.zeros_like(h_carry)


def _lru_phases(xp, g, shift, scale, gate, x1_buf, weights, scratch):
    wy_ref, by_ref, win_ref, bin_ref, cw_ref, cb_ref, wax_ref, bax_ref, lam_ref, wout_ref, \
        bout_ref = weights
    hp_buf, xbuf, xc_buf, mm_buf, a_buf, u_buf, start_buf, h_carry = scratch
    ts, width = a_buf.shape
    d = hp_buf.shape[1]
    heads, hd, _ = wax_ref.shape
    halo = CONV_HALO * SUBLANES

    gs = g * (1.0 + scale)
    for part in _splits(ts, 4):
        for r0 in range(part.start, part.stop, 2 * SUBLANES):
            rows = slice(r0, r0 + 2 * SUBLANES)
            xb = xp[rows, :]
            hp_buf[rows, :] = ((xb * _rms_scale(xb)) * gs + shift).astype(BF16)
        yield 164, (hp_buf,)

    for cols in _splits(width, 2):
        n = cols.stop - cols.start
        row = lax.broadcasted_iota(jnp.int32, (SUBLANES, n), 0)
        prev_tail = xbuf[ts:ts + halo, cols]
        xr = jnp.dot(hp_buf[...], win_ref[:, cols], preferred_element_type=F32)
        xbuf[halo:halo + ts, cols] = xr
        for p in range(CONV_HALO):
            blk = slice(p * SUBLANES, (p + 1) * SUBLANES)
            cur = xr[ts - halo + p * SUBLANES:ts - halo + (p + 1) * SUBLANES, :]
            xbuf[blk, cols] = _shift_segments(prev_tail[blk, :], cur, row)
        yield 0, (xbuf,)
    for cols in _splits(width, 4):
        taps = [cw_ref[k:k + 1, cols] for k in range(CONV_WIDTH)]
        part = cb_ref[:, cols] + bin_ref[:, cols] * sum(taps[1:], taps[0])
        for k in range(CONV_WIDTH):
            part = part + xbuf[k * SUBLANES:k * SUBLANES + ts, cols] * taps[k]
        xc_buf[:, cols] = part
        yield 256, (xc_buf,)

    half_log = (0.5 * LRU_C) * jax.nn.log_sigmoid(lam_ref[...])

    def gate_matmul(hh):
        park = slice((hh % 2) * 2 * hd, (hh % 2 + 1) * 2 * hd)
        mm_buf[:, park] = jnp.dot(xc_buf[:, hh * hd:(hh + 1) * hd].astype(BF16), wax_ref[hh],
                                  preferred_element_type=F32) + bax_ref(hh)

    def gate_math(hh, sub, rows):
        base = (hh % 2) * 2 * hd
        cols = slice(hh * hd + sub.start, hh * hd + sub.stop)
        gate_i = _sigmoid_of_twice(mm_buf[rows, base + hd + sub.start:base + hd + sub.stop])
        log_a = (jnp.tanh(mm_buf[rows, base + sub.start:base + sub.stop]) * half_log[:, cols]
                 + half_log[:, cols])
        a_buf[rows, cols] = jnp.exp(log_a)
        t = jnp.tanh(log_a)
        q = -2.0 * t
        mult = jnp.where(q > 0.0, q * lax.rsqrt(q * (1.0 - t)), 0.0)
        u_buf[rows, cols] = mult * (gate_i * xc_buf[rows, cols])

    for first in range(0, heads, 2):
        for hh in range(first, min(first + 2, heads)):
            gate_matmul(hh)
            yield 0, (mm_buf,)
        for hh in range(first, min(first + 2, heads)):
            for sub in _splits(hd, hd // LANES):
                for rows in _splits(ts, 2):
                    gate_math(hh, sub, rows)
                    yield 240, (a_buf, u_buf)

    for cols in _splits(width, 2):
        n = cols.stop - cols.start
        row = lax.broadcasted_iota(jnp.int32, (SUBLANES, n), 0)
        seg_u = jnp.zeros((SUBLANES, n), F32)
        seg_a = jnp.ones((SUBLANES, n), F32)
        for j in range(SEG_LEN):
            blk = slice(j * SUBLANES, (j + 1) * SUBLANES)
            a = a_buf[blk, cols]
            seg_u = a * seg_u + u_buf[blk, cols]
            seg_a = a * seg_a
            u_buf[blk, cols] = seg_u
            a_buf[blk, cols] = seg_a
        k = 1
        while k < SUBLANES:
            keep = row >= k
            a_prev = jnp.where(keep, pltpu.roll(seg_a, k, 0), 1.0)
            u_prev = jnp.where(keep, pltpu.roll(seg_u, k, 0), 0.0)
            seg_u = seg_u + seg_a * u_prev
            seg_a = seg_a * a_prev
            k *= 2
        state_in = h_carry[0:1, cols]
        seg_end = seg_a * state_in + seg_u
        h_carry[0:1, cols] = seg_end[SUBLANES - 1:SUBLANES, :]
        start_buf[:, cols] = jnp.where(row == 0, state_in, pltpu.roll(seg_end, 1, 0))
        yield 200, (a_buf, u_buf)

    for cols in _splits(width, 2):
        mm_buf[:, cols] = (jnp.dot(hp_buf[...], wy_ref[:, cols], preferred_element_type=F32)
                           + by_ref[:, cols])
        yield 0, (mm_buf,)
    for cols in _splits(width, width // LANES):
        n = cols.stop - cols.start
        hs = (u_buf[:, cols].reshape(SEG_LEN, SUBLANES, n)
              + a_buf[:, cols].reshape(SEG_LEN, SUBLANES, n) * start_buf[:, cols][None]
              ).reshape(ts, n)
        hp_buf[:, cols] = (hs * _gelu_tanh(mm_buf[:, cols])).astype(BF16)
        yield 180, (hp_buf,)

    for cols in _splits(d, 2):
        y = jnp.dot(hp_buf[...], wout_ref[:, cols], preferred_element_type=F32) + bout_ref[:, cols]
        x1_buf[:, cols] = xp[:, cols] + gate[:, cols] * y
        yield 64, (x1_buf,)


def _lru_costs(heads, hd, width):
    per_pair = [0, 0] + [240] * (2 * 2 * (hd // LANES))
    return ([164] * 4 + [0] * 2 + [256] * 4 + per_pair * (heads // 2)
            + [200] * 2 + [0] * 2 + [180] * (width // LANES) + [64] * 2)


def _pool_reset(scratch):
    (hbuf,) = scratch
    ts = hbuf.shape[0] - POOL_HALO * SUBLANES
    hbuf[ts:, :] = jnp.zeros((POOL_HALO * SUBLANES, hbuf.shape[1]), F32)


def _pool_phases(xp, g, shift, scale, gate, x1_buf, weights, scratch, tile_start):
    pw_ref, ps_ref = weights
    (hbuf,) = scratch
    halo = POOL_HALO * SUBLANES
    ts = hbuf.shape[0] - halo
    d = hbuf.shape[1]
    groups, gd, _ = pw_ref.shape

    gs = g * (1.0 + scale)
    row = lax.broadcasted_iota(jnp.int32, (SUBLANES, d), 0)
    parts = _splits(ts, 4)
    assert parts[-1].start == ts - halo
    for part in (parts[-1],) + parts[:-1]:
        for r0 in range(part.start, part.stop, 2 * SUBLANES):
            rows = slice(halo + r0, halo + r0 + 2 * SUBLANES)
            xb = xp[r0:r0 + 2 * SUBLANES, :]
            hb = (xb * _rms_scale(xb)) * gs + shift
            if r0 >= ts - halo:
                old = hbuf[rows, :]
                for i in range(2):
                    blk = slice(i * SUBLANES, (i + 1) * SUBLANES)
                    p0 = r0 - (ts - halo) + i * SUBLANES
                    hbuf[p0:p0 + SUBLANES, :] = _shift_segments(old[blk, :], hb[blk, :], row)
            hbuf[rows, :] = hb
        yield 190, (hbuf,)

    m = lax.broadcasted_iota(jnp.int32, (ts, 1), 0)
    pos = tile_start + (m % SUBLANES) * SEG_LEN + m // SUBLANES + 1
    for gi, win in enumerate(POOL_WINDOWS):
        cols = slice(gi * gd, (gi + 1) * gd)
        total = hbuf[(POOL_HALO - (win - 1)) * SUBLANES:, cols]
        step = 1
        while step < win:
            total = total[step * SUBLANES:, :] + total[:-step * SUBLANES, :]
            step *= 2
        inv_count = 1.0 / jnp.minimum(pos, win).astype(F32)
        pooled = total * inv_count - hbuf[halo:, cols]
        mixed = jnp.dot(pooled.astype(BF16), pw_ref[gi], preferred_element_type=F32)
        x1_buf[:, cols] = xp[:, cols] + gate[:, cols] * (mixed * ps_ref[:, cols])
        yield 280, (x1_buf,)


def _pool_costs(groups):
    return [190] * 4 + [280] * groups


PERMUTE_IN_COSTS = [0] * 4 + [70] * 2
HAND_OVER_COSTS = [190, 190, 190, 190]


def _layer_kernel(*refs, kind, first, final, tiles_per_seq, n_mixer_weights,
                  n_mixer_scratch):
    x_ref, modm_ref, modf_ref, vec_ref = refs[:4]
    mixer_mats = refs[4:4 + n_mixer_weights]
    w1_ref, w2_ref, o_ref = refs[4 + n_mixer_weights:7 + n_mixer_weights]
    rest = refs[7 + n_mixer_weights:]
    gm_ref, gf_ref = _Rows(vec_ref, 0), _Rows(vec_ref, 1)
    if kind == "lru":
        wy_ref, win_ref, wax_ref, wout_ref = mixer_mats
        heads, hd, _ = wax_ref.shape
        per_row = vec_ref.shape[1] // (2 * hd)
        mixer_weights = (
            wy_ref, _Rows(vec_ref, 2), win_ref, _Rows(vec_ref, 3), _Rows(vec_ref, 4, CONV_WIDTH),
            _Rows(vec_ref, 8), wax_ref,
            lambda hh: vec_ref[11 + hh // per_row:12 + hh // per_row,
                               (hh % per_row) * 2 * hd:(hh % per_row + 1) * 2 * hd],
            _Rows(vec_ref, 9), wout_ref, _Rows(vec_ref, 10))
        fg_ref = _Rows(vec_ref, 13)
    else:
        mixer_weights = (mixer_mats[0], _Rows(vec_ref, 2))
        fg_ref = _Rows(vec_ref, 3)
    x1_buf, h1_buf = rest[:2]
    rest = rest[2:]
    slab = xp_buf = None
    if first or final:
        slab, rest = rest[0], rest[1:]
    if first:
        xp_buf, rest = rest[0], rest[1:]
    mixer_scratch = rest[:n_mixer_scratch]
    ts, d = x_ref.shape
    g = pl.program_id(0)

    def run_if(cond, body, trips=1):
        lax.fori_loop(0, jnp.where(cond, trips, 0), lambda i, c: (body(i), c)[1], 0)

    def clear_rows(i):
        pair = 2 * SUBLANES
        rows = pl.ds(pl.multiple_of(i * pair, pair), pair)
        x1_buf[rows, :] = jnp.zeros((pair, d), F32)
        h1_buf[rows, :] = jnp.zeros((pair, d), BF16)

    run_if(g == 0, clear_rows, trips=ts // (2 * SUBLANES))
    run_if(g % tiles_per_seq == 0,
           lambda i: (_lru_reset(mixer_scratch, mixer_weights[3]) if kind == "lru"
                      else _pool_reset(mixer_scratch)))

    xp = xp_buf if first else x_ref
    norm_args = (xp, gm_ref[...], modm_ref[0:1, :], modm_ref[1:2, :], modm_ref[2:3, :], x1_buf)
    if kind == "lru":
        heads, hd, _ = mixer_weights[6].shape
        assert heads % 2 == 0 and 4 * hd <= d
        mixer = _lru_phases(*norm_args, mixer_weights, mixer_scratch)
        costs = _lru_costs(heads, hd, d)
    else:
        mixer = _pool_phases(*norm_args, mixer_weights, mixer_scratch,
                             (g % tiles_per_seq) * ts)
        costs = _pool_costs(mixer_weights[0].shape[0])
    costs = (PERMUTE_IN_COSTS if first else []) + costs + HAND_OVER_COSTS

    def all_phases():
        if first:
            yield from _permute_in(x_ref, slab, xp_buf)
        yield from mixer
        gs = gf_ref[...] * (1.0 + modm_ref[4:5, :])
        for part in _splits(ts, 4):
            for r0 in range(part.start, part.stop, 2 * SUBLANES):
                rows = slice(r0, r0 + 2 * SUBLANES)
                x1 = x1_buf[rows, :]
                h1_buf[rows, :] = (
                    (x1 * _rms_scale(x1)) * gs + modm_ref[3:4, :]).astype(BF16)
            yield 190, (h1_buf,)

    phases = all_phases()
    total_cost = float(sum(costs) - sum(HAND_OVER_COSTS))
    spent = 0.0
    n_emitted = 0
    written = []

    def fill(fraction):
        nonlocal spent, n_emitted
        while n_emitted < len(costs) and spent < fraction * total_cost:
            cost, refs = next(phases)
            assert cost == costs[n_emitted], (n_emitted, cost, costs[n_emitted])
            spent += cost
            n_emitted += 1
            written.extend(r for r in refs if all(r is not w for w in written))

    def anchor(target):
        pair = 2 * SUBLANES
        rows = pl.ds(pl.multiple_of(lax.shift_right_logical(g, 30) * pair, pair), pair)
        bits = None
        for ref in written:
            tok = ref[0, rows, :] if len(ref.shape) == 3 else ref[rows, 0:LANES]
            b = pltpu.bitcast(tok, jnp.uint32)
            if b.shape[0] == pair:
                b = b[:SUBLANES] | b[SUBLANES:]
            bits = b if bits is None else bits | b
        del written[:]
        if bits is not None:
            zero = pltpu.bitcast(lax.shift_right_logical(bits, jnp.uint32(32)), BF16)
            target[0:pair, 0:LANES] = target[0:pair, 0:LANES] + zero

    d_ff = w1_ref.shape[1]
    n_chunks = d_ff // FF_CHUNK
    n_gaps = 2 * n_chunks - 1
    acc = None
    for k in range(n_chunks):
        if k > 0:
            anchor(h1_buf)
        cols = slice(k * FF_CHUNK, (k + 1) * FF_CHUNK)
        u = jnp.maximum(jnp.dot(h1_buf[...], w1_ref[:, cols], preferred_element_type=F32), 0.0)
        u = (u * u).astype(BF16)
        if k == 0:
            o_ref[...] = x1_buf[...]
        fill((2 * k + 1) / n_gaps)
        p = jnp.dot(u, w2_ref[cols, :], preferred_element_type=F32)
        acc = p if acc is None else acc + p
        fill((2 * k + 2) / n_gaps if k + 1 < n_chunks else 2.0)
    assert n_emitted == len(costs) and next(phases, "done") == "done"
    out = o_ref[...] + modf_ref[5:6, :] * acc
    if final:
        out = (out * _rms_scale(out)) * fg_ref[...]
        _permute_out(out, slab, o_ref)
    else:
        o_ref[...] = out


def _layer(x, mod, layer, kind, vec, mixer_mats, j, w1, w2, first, final):
    n_tiles, ts, d = x.shape
    bsz = mod.shape[1]
    tiles_per_seq = n_tiles // bsz
    last = n_tiles - 1

    def resident(op, idx):
        nd = op.ndim - 1
        return pl.BlockSpec((None,) + op.shape[1:], lambda g: (idx,) + (0,) * nd,
                            pipeline_mode=pl.Buffered(1))

    def mix_tile(g):
        return jnp.minimum(g, last)

    def mlp_tile(g):
        return jnp.maximum(g - 1, 0)

    operands = [x, mod, mod, vec, *mixer_mats, w1, w2]
    in_specs = [
        pl.BlockSpec((None, ts, d), lambda g: (mix_tile(g), 0, 0)),
        pl.BlockSpec((None, None, N_MOD, d),
                     lambda g: (layer, mix_tile(g) // tiles_per_seq, 0, 0)),
        pl.BlockSpec((None, None, N_MOD, d),
                     lambda g: (layer, mlp_tile(g) // tiles_per_seq, 0, 0)),
        resident(vec, j),
        *[resident(op, j) for op in mixer_mats],
        resident(w1, layer), resident(w2, layer),
    ]

    scratch = [
        pltpu.VMEM((ts, d), F32),
        pltpu.VMEM((ts, d), BF16),
    ]
    if first or final:
        scratch.append(pltpu.VMEM((d // LANES, SUBLANES * SEG_PITCH, LANES), F32))
    if first:
        scratch.append(pltpu.VMEM((ts, d), F32))
    if kind == "lru":
        width = mixer_mats[0].shape[2]
        assert width == d
        mixer_scratch = [
            pltpu.VMEM((ts, d), BF16),
            pltpu.VMEM((ts + CONV_HALO * SUBLANES, width), F32),
            pltpu.VMEM((ts, width), F32),
            pltpu.VMEM((ts, width), F32),
            pltpu.VMEM((ts, width), F32),
            pltpu.VMEM((ts, width), F32),
            pltpu.VMEM((SUBLANES, width), F32),
            pltpu.VMEM((SUBLANES, width), F32),
        ]
    else:
        mixer_scratch = [pltpu.VMEM((ts + POOL_HALO * SUBLANES, d), F32)]

    return pl.pallas_call(
        functools.partial(_layer_kernel, kind=kind, first=first, final=final,
                          tiles_per_seq=tiles_per_seq, n_mixer_weights=len(mixer_mats),
                          n_mixer_scratch=len(mixer_scratch)),
        grid=(n_tiles + 1,),
        in_specs=in_specs,
        out_specs=pl.BlockSpec((None, ts, d), lambda g: (mlp_tile(g), 0, 0)),
        out_shape=jax.ShapeDtypeStruct(x.shape, F32),
        scratch_shapes=scratch + mixer_scratch,
        compiler_params=pltpu.CompilerParams(
            dimension_semantics=("arbitrary",), vmem_limit_bytes=VMEM_LIMIT),
        name=kind + "_layer" + ("_first" if first else "") + ("_final" if final else ""),
    )(*operands)


def kernel(x, c, w_mod, b_mod, norm_mix_g, norm_ffn_g, lru_w_y, lru_b_y, lru_w_in, lru_b_in,
           lru_conv_w, lru_conv_b, lru_w_a, lru_b_a, lru_w_x, lru_b_x, lru_lambda, lru_w_out,
           lru_b_out, pool_w, pool_scale, ffn_w1, ffn_w2, final_norm_g):
    depth = w_mod.shape[0]
    bsz, seq, d = x.shape
    assert seq % SEQ_TILE == 0 and ffn_w1.shape[2] % FF_CHUNK == 0
    assert w_mod.shape[2] % MOD_COLS == 0 and d % (4 * LANES) == 0
    mod = _modulation(c, w_mod, b_mod)

    n_a, heads, hd, _ = lru_w_a.shape
    n_b = pool_w.shape[0]
    w_ax = (0.5 * jnp.concatenate([lru_w_a, lru_w_x], axis=-1)).astype(BF16)
    b_ax = (0.5 * jnp.concatenate([lru_b_a, lru_b_x], axis=-1)).reshape(n_a, -1, d)
    lru_mats = (lru_w_y.astype(BF16), lru_w_in.astype(BF16), w_ax, lru_w_out.astype(BF16))
    pool_mats = (pool_w.astype(BF16),)
    w1, w2 = ffn_w1.astype(BF16), ffn_w2.astype(BF16)

    def rows(v):
        return v[:, None, :]

    def last_norm(n):
        return jnp.broadcast_to(final_norm_g, (n, 1, d))

    assert b_ax.shape[1] == 2
    vec_lru = jnp.concatenate(
        [rows(norm_mix_g[0::N_MIXERS]), rows(norm_ffn_g[0::N_MIXERS]), rows(lru_b_y),
         rows(lru_b_in), lru_conv_w, rows(lru_conv_b), rows(lru_lambda), rows(lru_b_out), b_ax,
         last_norm(n_a), jnp.zeros((n_a, 2, d), F32)], axis=1)
    vec_pool = jnp.concatenate(
        [rows(norm_mix_g[1::N_MIXERS]), rows(norm_ffn_g[1::N_MIXERS]), rows(pool_scale),
         last_norm(n_b)], axis=1)

    x = x.reshape(bsz * seq // SEQ_TILE, SEQ_TILE, d)
    for i in range(depth):
        j = i // N_MIXERS
        if i % N_MIXERS == 0:
            kind, vec, mats = "lru", vec_lru, lru_mats
        else:
            kind, vec, mats = "pool", vec_pool, pool_mats
        x = _layer(x, mod, i, kind, vec, mats, j, w1, w2, i == 0, i == depth - 1)
    return x.reshape(bsz, seq, d)
```

```python
import functools
import math

import jax
import jax.numpy as jnp
from jax import lax
from jax.experimental import pallas as pl
from jax.experimental.pallas import tpu as pltpu

F32 = jnp.float32
BF16 = jnp.bfloat16

N_MIXERS = 2
N_MOD = 6
CONV_WIDTH = 4
LRU_C = 8.0
POOL_WINDOWS = (2, 4, 8, 16)
EPS = 1e-6

SUBLANES = 8
LANES = 128
SEQ_TILE = 512
SEG_LEN = SEQ_TILE // SUBLANES
SEG_PITCH = SEG_LEN + SUBLANES
FF_CHUNK = 1024
MOD_COLS = 1536
CONV_HALO = CONV_WIDTH - 1
POOL_HALO = max(POOL_WINDOWS)
VMEM_LIMIT = 58 * 1024 * 1024

GELU_C1 = math.sqrt(2.0 / math.pi)
GELU_C2 = GELU_C1 * 0.044715


def _rms_scale(x):
    return lax.rsqrt(jnp.mean(x * x, axis=-1, keepdims=True) + EPS)


def _splits(n, parts):
    step = n // parts
    return tuple(slice(i * step, (i + 1) * step) for i in range(parts))


class _Rows:
    def __init__(self, ref, first, count=1):
        self.ref, self.first, self.count = ref, first, count

    def __getitem__(self, idx):
        if idx is Ellipsis:
            return self.ref[self.first:self.first + self.count, :]
        rows, cols = idx
        start = self.first + (rows.start or 0)
        stop = self.first + (self.count if rows.stop is None else rows.stop)
        return self.ref[start:stop, cols]


def _mod_kernel(c_ref, w_ref, b_ref, o_ref):
    c = c_ref[...]
    cond = c * jax.nn.sigmoid(c)
    o_ref[...] = jnp.dot(cond.astype(BF16), w_ref[...].astype(BF16),
                         preferred_element_type=F32) + b_ref[...]


def _modulation(c, w_mod, b_mod):
    depth, d, e = w_mod.shape
    bsz = c.shape[0]
    rows = -(-bsz // SUBLANES) * SUBLANES
    c_pad = jnp.zeros((rows, d), F32).at[:bsz].set(c)
    out = pl.pallas_call(
        _mod_kernel,
        grid=(depth, e // MOD_COLS),
        in_specs=[
            pl.BlockSpec((rows, d), lambda i, j: (0, 0)),
            pl.BlockSpec((None, d, MOD_COLS), lambda i, j: (i, 0, j)),
            pl.BlockSpec((None, 1, MOD_COLS), lambda i, j: (i, 0, j)),
        ],
        out_specs=pl.BlockSpec((None, rows, MOD_COLS), lambda i, j: (i, 0, j)),
        out_shape=jax.ShapeDtypeStruct((depth, rows, e), F32),
        compiler_params=pltpu.CompilerParams(
            dimension_semantics=("arbitrary", "arbitrary"), vmem_limit_bytes=VMEM_LIMIT),
        name="adaln_mod",
    )(c_pad, w_mod, b_mod.reshape(depth, 1, e))
    return out[:, :bsz].reshape(depth, bsz, N_MOD, d)


def _permute_in(x_ref, slab, xp_buf):
    d = x_ref.shape[1]
    for cols in _splits(d, 4):
        for c in range(cols.start // LANES, cols.stop // LANES):
            for r in range(SUBLANES):
                slab[c, r * SEG_PITCH:r * SEG_PITCH + SEG_LEN, :] = (
                    x_ref[r * SEG_LEN:(r + 1) * SEG_LEN, c * LANES:(c + 1) * LANES])
        yield 0, (slab,)
    for js in _splits(SEG_LEN, 2):
        for j in range(js.start, js.stop):
            for c in range(d // LANES):
                xp_buf[j * SUBLANES:(j + 1) * SUBLANES, c * LANES:(c + 1) * LANES] = (
                    slab[c, pl.ds(j, SUBLANES, stride=SEG_PITCH), :])
        yield 70, (xp_buf,)


def _permute_out(y, slab, o_ref):
    d = y.shape[1]
    for c in range(d // LANES):
        slab[c, 0:y.shape[0], :] = y[:, c * LANES:(c + 1) * LANES]
    for q in range(SEG_LEN // SUBLANES):
        for r in range(SUBLANES):
            src = q * SUBLANES * SUBLANES + r
            dst = slice(r * SEG_LEN + q * SUBLANES, r * SEG_LEN + (q + 1) * SUBLANES)
            for c in range(d // LANES):
                o_ref[dst, c * LANES:(c + 1) * LANES] = (
                    slab[c, pl.ds(src, SUBLANES, stride=SUBLANES), :])


def _shift_segments(prev_block, cur_block, row):
    return jnp.where(row == 0, pltpu.roll(prev_block, 1, 0), pltpu.roll(cur_block, 1, 0))


def _sigmoid_of_twice(v):
    return 0.5 * jnp.tanh(v) + 0.5


def _gelu_tanh(v):
    return (0.5 * v) * (1.0 + jnp.tanh(v * (GELU_C1 + GELU_C2 * (v * v))))


def _lru_reset(scratch, b_in):
    hp_buf, xbuf, xc_buf, mm_buf, a_buf, u_buf, start_buf, h_carry = scratch
    ts, width = a_buf.shape
    xbuf[ts:ts + CONV_HALO * SUBLANES, :] = jnp.broadcast_to(
        -b_in[...], (CONV_HALO * SUBLANES, width))
    h_carry[...] = jnp.zeros_like(h_carry)


def _lru_phases(xp, g, shift, scale, gate, x1_buf, weights, scratch):
    wy_ref, by_ref, win_ref, bin_ref, cw_ref, cb_ref, wax_ref, bax_ref, lam_ref, wout_ref, \
        bout_ref = weights
    hp_buf, xbuf, xc_buf, mm_buf, a_buf, u_buf, start_buf, h_carry = scratch
    ts, width = a_buf.shape
    d = hp_buf.shape[1]
    heads, hd, _ = wax_ref.shape
    halo = CONV_HALO * SUBLANES

    gs = g * (1.0 + scale)
    for part in _splits(ts, 4):
        for r0 in range(part.start, part.stop, 2 * SUBLANES):
            rows = slice(r0, r0 + 2 * SUBLANES)
            xb = xp[rows, :]
            hp_buf[rows, :] = ((xb * _rms_scale(xb)) * gs + shift).astype(BF16)
        yield 164, (hp_buf,)

    for cols in _splits(width, 2):
        n = cols.stop - cols.start
        row = lax.broadcasted_iota(jnp.int32, (SUBLANES, n), 0)
        prev_tail = xbuf[ts:ts + halo, cols]
        xr = jnp.dot(hp_buf[...], win_ref[:, cols], preferred_element_type=F32)
        xbuf[halo:halo + ts, cols] = xr
        for p in range(CONV_HALO):
            blk = slice(p * SUBLANES, (p + 1) * SUBLANES)
            cur = xr[ts - halo + p * SUBLANES:ts - halo + (p + 1) * SUBLANES, :]
            xbuf[blk, cols] = _shift_segments(prev_tail[blk, :], cur, row)
        yield 0, (xbuf,)
    for cols in _splits(width, 4):
        taps = [cw_ref[k:k + 1, cols] for k in range(CONV_WIDTH)]
        part = cb_ref[:, cols] + bin_ref[:, cols] * sum(taps[1:], taps[0])
        for k in range(CONV_WIDTH):
            part = part + xbuf[k * SUBLANES:k * SUBLANES + ts, cols] * taps[k]
        xc_buf[:, cols] = part
        yield 256, (xc_buf,)

    half_log = (0.5 * LRU_C) * jax.nn.log_sigmoid(lam_ref[...])

    def gate_matmul(hh):
        park = slice((hh % 2) * 2 * hd, (hh % 2 + 1) * 2 * hd)
        mm_buf[:, park] = jnp.dot(xc_buf[:, hh * hd:(hh + 1) * hd].astype(BF16), wax_ref[hh],
                                  preferred_element_type=F32) + bax_ref(hh)

    def gate_math(hh, sub, rows):
        base = (hh % 2) * 2 * hd
        cols = slice(hh * hd + sub.start, hh * hd + sub.stop)
        gate_i = _sigmoid_of_twice(mm_buf[rows, base + hd + sub.start:base + hd + sub.stop])
        log_a = (jnp.tanh(mm_buf[rows, base + sub.start:base + sub.stop]) * half_log[:, cols]
                 + half_log[:, cols])
        a_buf[rows, cols] = jnp.exp(log_a)
        t = jnp.tanh(log_a)
        q = -2.0 * t
        mult = jnp.where(q > 0.0, q * lax.rsqrt(q * (1.0 - t)), 0.0)
        u_buf[rows, cols] = mult * (gate_i * xc_buf[rows, cols])

    for first in range(0, heads, 2):
        for hh in range(first, min(first + 2, heads)):
            gate_matmul(hh)
            yield 0, (mm_buf,)
        for hh in range(first, min(first + 2, heads)):
            for sub in _splits(hd, hd // LANES):
                for rows in _splits(ts, 2):
                    gate_math(hh, sub, rows)
                    yield 240, (a_buf, u_buf)

    for cols in _splits(width, 2):
        n = cols.stop - cols.start
        row = lax.broadcasted_iota(jnp.int32, (SUBLANES, n), 0)
        seg_u = jnp.zeros((SUBLANES, n), F32)
        seg_a = jnp.ones((SUBLANES, n), F32)
        for j in range(SEG_LEN):
            blk = slice(j * SUBLANES, (j + 1) * SUBLANES)
            a = a_buf[blk, cols]
            seg_u = a * seg_u + u_buf[blk, cols]
            seg_a = a * seg_a
            u_buf[blk, cols] = seg_u
            a_buf[blk, cols] = seg_a
        k = 1
        while k < SUBLANES:
            keep = row >= k
            a_prev = jnp.where(keep, pltpu.roll(seg_a, k, 0), 1.0)
            u_prev = jnp.where(keep, pltpu.roll(seg_u, k, 0), 0.0)
            seg_u = seg_u + seg_a * u_prev
            seg_a = seg_a * a_prev
            k *= 2
        state_in = h_carry[0:1, cols]
        seg_end = seg_a * state_in + seg_u
        h_carry[0:1, cols] = seg_end[SUBLANES - 1:SUBLANES, :]
        start_buf[:, cols] = jnp.where(row == 0, state_in, pltpu.roll(seg_end, 1, 0))
        yield 200, (a_buf, u_buf)

    for cols in _splits(width, 2):
        mm_buf[:, cols] = (jnp.dot(hp_buf[...], wy_ref[:, cols], preferred_element_type=F32)
                           + by_ref[:, cols])
        yield 0, (mm_buf,)
    for cols in _splits(width, width // LANES):
        n = cols.stop - cols.start
        hs = (u_buf[:, cols].reshape(SEG_LEN, SUBLANES, n)
              + a_buf[:, cols].reshape(SEG_LEN, SUBLANES, n) * start_buf[:, cols][None]
              ).reshape(ts, n)
        hp_buf[:, cols] = (hs * _gelu_tanh(mm_buf[:, cols])).astype(BF16)
        yield 180, (hp_buf,)

    for cols in _splits(d, 2):
        y = jnp.dot(hp_buf[...], wout_ref[:, cols], preferred_element_type=F32) + bout_ref[:, cols]
        x1_buf[:, cols] = xp[:, cols] + gate[:, cols] * y
        yield 64, (x1_buf,)


def _lru_costs(heads, hd, width):
    per_pair = [0, 0] + [240] * (2 * 2 * (hd // LANES))
    return ([164] * 4 + [0] * 2 + [256] * 4 + per_pair * (heads // 2)
            + [200] * 2 + [0] * 2 + [180] * (width // LANES) + [64] * 2)


def _pool_reset(scratch):
    (hbuf,) = scratch
    ts = hbuf.shape[0] - POOL_HALO * SUBLANES
    hbuf[ts:, :] = jnp.zeros((POOL_HALO * SUBLANES, hbuf.shape[1]), F32)


def _pool_phases(xp, g, shift, scale, gate, x1_buf, weights, scratch, tile_start):
    pw_ref, ps_ref = weights
    (hbuf,) = scratch
    halo = POOL_HALO * SUBLANES
    ts = hbuf.shape[0] - halo
    d = hbuf.shape[1]
    groups, gd, _ = pw_ref.shape

    gs = g * (1.0 + scale)
    row = lax.broadcasted_iota(jnp.int32, (SUBLANES, d), 0)
    parts = _splits(ts, 4)
    assert parts[-1].start == ts - halo
    for part in (parts[-1],) + parts[:-1]:
        for r0 in range(part.start, part.stop, 2 * SUBLANES):
            rows = slice(halo + r0, halo + r0 + 2 * SUBLANES)
            xb = xp[r0:r0 + 2 * SUBLANES, :]
            hb = (xb * _rms_scale(xb)) * gs + shift
            if r0 >= ts - halo:
                old = hbuf[rows, :]
                for i in range(2):
                    blk = slice(i * SUBLANES, (i + 1) * SUBLANES)
                    p0 = r0 - (ts - halo) + i * SUBLANES
                    hbuf[p0:p0 + SUBLANES, :] = _shift_segments(old[blk, :], hb[blk, :], row)
            hbuf[rows, :] = hb
        yield 190, (hbuf,)

    m = lax.broadcasted_iota(jnp.int32, (ts, 1), 0)
    pos = tile_start + (m % SUBLANES) * SEG_LEN + m // SUBLANES + 1
    for gi, win in enumerate(POOL_WINDOWS):
        cols = slice(gi * gd, (gi + 1) * gd)
        total = hbuf[(POOL_HALO - (win - 1)) * SUBLANES:, cols]
        step = 1
        while step < win:
            total = total[step * SUBLANES:, :] + total[:-step * SUBLANES, :]
            step *= 2
        inv_count = 1.0 / jnp.minimum(pos, win).astype(F32)
        pooled = total * inv_count - hbuf[halo:, cols]
        mixed = jnp.dot(pooled.astype(BF16), pw_ref[gi], preferred_element_type=F32)
        x1_buf[:, cols] = xp[:, cols] + gate[:, cols] * (mixed * ps_ref[:, cols])
        yield 280, (x1_buf,)


def _pool_costs(groups):
    return [190] * 4 + [280] * groups


PERMUTE_IN_COSTS = [0] * 4 + [70] * 2
HAND_OVER_COSTS = [190, 190, 190, 190]


def _layer_kernel(*refs, kind, first, final, tiles_per_seq, n_mixer_weights,
                  n_mixer_scratch):
    x_ref, modm_ref, modf_ref, vec_ref = refs[:4]
    mixer_mats = refs[4:4 + n_mixer_weights]
    w1_ref, w2_ref, o_ref = refs[4 + n_mixer_weights:7 + n_mixer_weights]
    rest = refs[7 + n_mixer_weights:]
    gm_ref, gf_ref = _Rows(vec_ref, 0), _Rows(vec_ref, 1)
    if kind == "lru":
        wy_ref, win_ref, wax_ref, wout_ref = mixer_mats
        heads, hd, _ = wax_ref.shape
        per_row = vec_ref.shape[1] // (2 * hd)
        mixer_weights = (
            wy_ref, _Rows(vec_ref, 2), win_ref, _Rows(vec_ref, 3), _Rows(vec_ref, 4, CONV_WIDTH),
            _Rows(vec_ref, 8), wax_ref,
            lambda hh: vec_ref[11 + hh // per_row:12 + hh // per_row,
                               (hh % per_row) * 2 * hd:(hh % per_row + 1) * 2 * hd],
            _Rows(vec_ref, 9), wout_ref, _Rows(vec_ref, 10))
        fg_ref = _Rows(vec_ref, 13)
    else:
        mixer_weights = (mixer_mats[0], _Rows(vec_ref, 2))
        fg_ref = _Rows(vec_ref, 3)
    x1_buf, h1_buf = rest[:2]
    rest = rest[2:]
    slab = xp_buf = None
    if first or final:
        slab, rest = rest[0], rest[1:]
    if first:
        xp_buf, rest = rest[0], rest[1:]
    mixer_scratch = rest[:n_mixer_scratch]
    ts, d = x_ref.shape
    g = pl.program_id(0)

    def run_if(cond, body, trips=1):
        lax.fori_loop(0, jnp.where(cond, trips, 0), lambda i, c: (body(i), c)[1], 0)

    def clear_rows(i):
        pair = 2 * SUBLANES
        rows = pl.ds(pl.multiple_of(i * pair, pair), pair)
        x1_buf[rows, :] = jnp.zeros((pair, d), F32)
        h1_buf[rows, :] = jnp.zeros((pair, d), BF16)

    run_if(g == 0, clear_rows, trips=ts // (2 * SUBLANES))
    run_if(g % tiles_per_seq == 0,
           lambda i: (_lru_reset(mixer_scratch, mixer_weights[3]) if kind == "lru"
                      else _pool_reset(mixer_scratch)))

    xp = xp_buf if first else x_ref
    norm_args = (xp, gm_ref[...], modm_ref[0:1, :], modm_ref[1:2, :], modm_ref[2:3, :], x1_buf)
    if kind == "lru":
        heads, hd, _ = mixer_weights[6].shape
        assert heads % 2 == 0 and 4 * hd <= d
        mixer = _lru_phases(*norm_args, mixer_weights, mixer_scratch)
        costs = _lru_costs(heads, hd, d)
    else:
        mixer = _pool_phases(*norm_args, mixer_weights, mixer_scratch,
                             (g % tiles_per_seq) * ts)
        costs = _pool_costs(mixer_weights[0].shape[0])
    costs = (PERMUTE_IN_COSTS if first else []) + costs + HAND_OVER_COSTS

    def all_phases():
        if first:
            yield from _permute_in(x_ref, slab, xp_buf)
        yield from mixer
        gs = gf_ref[...] * (1.0 + modm_ref[4:5, :])
        for part in _splits(ts, 4):
            for r0 in range(part.start, part.stop, 2 * SUBLANES):
                rows = slice(r0, r0 + 2 * SUBLANES)
                x1 = x1_buf[rows, :]
                h1_buf[rows, :] = (
                    (x1 * _rms_scale(x1)) * gs + modm_ref[3:4, :]).astype(BF16)
            yield 190, (h1_buf,)

    phases = all_phases()
    total_cost = float(sum(costs) - sum(HAND_OVER_COSTS))
    spent = 0.0
    n_emitted = 0
    written = []

    def fill(fraction):
        nonlocal spent, n_emitted
        while n_emitted < len(costs) and spent < fraction * total_cost:
            cost, refs = next(phases)
            assert cost == costs[n_emitted], (n_emitted, cost, costs[n_emitted])
            spent += cost
            n_emitted += 1
            written.extend(r for r in refs if all(r is not w for w in written))

    def anchor(target):
        pair = 2 * SUBLANES
        rows = pl.ds(pl.multiple_of(lax.shift_right_logical(g, 30) * pair, pair), pair)
        bits = None
        for ref in written:
            tok = ref[0, rows, :] if len(ref.shape) == 3 else ref[rows, 0:LANES]
            b = pltpu.bitcast(tok, jnp.uint32)
            if b.shape[0] == pair:
                b = b[:SUBLANES] | b[SUBLANES:]
            bits = b if bits is None else bits | b
        del written[:]
        if bits is not None:
            zero = pltpu.bitcast(lax.shift_right_logical(bits, jnp.uint32(32)), BF16)
            target[0:pair, 0:LANES] = target[0:pair, 0:LANES] + zero

    d_ff = w1_ref.shape[1]
    n_chunks = d_ff // FF_CHUNK
    n_gaps = 2 * n_chunks - 1
    acc = None
    for k in range(n_chunks):
        if k > 0:
            anchor(h1_buf)
        cols = slice(k * FF_CHUNK, (k + 1) * FF_CHUNK)
        u = jnp.maximum(jnp.dot(h1_buf[...], w1_ref[:, cols], preferred_element_type=F32), 0.0)
        u = (u * u).astype(BF16)
        if k == 0:
            o_ref[...] = x1_buf[...]
        fill((2 * k + 1) / n_gaps)
        p = jnp.dot(u, w2_ref[cols, :], preferred_element_type=F32)
        acc = p if acc is None else acc + p
        fill((2 * k + 2) / n_gaps if k + 1 < n_chunks else 2.0)
    assert n_emitted == len(costs) and next(phases, "done") == "done"
    out = o_ref[...] + modf_ref[5:6, :] * acc
    if final:
        out = (out * _rms_scale(out)) * fg_ref[...]
        _permute_out(out, slab, o_ref)
    else:
        o_ref[...] = out


def _layer(x, mod, layer, kind, vec, mixer_mats, j, w1, w2, first, final):
    n_tiles, ts, d = x.shape
    bsz = mod.shape[1]
    tiles_per_seq = n_tiles // bsz
    last = n_tiles - 1

    def resident(op, idx):
        nd = op.ndim - 1
        return pl.BlockSpec((None,) + op.shape[1:], lambda g: (idx,) + (0,) * nd,
                            pipeline_mode=pl.Buffered(1))

    def mix_tile(g):
        return jnp.minimum(g, last)

    def mlp_tile(g):
        return jnp.maximum(g - 1, 0)

    operands = [x, mod, mod, vec, *mixer_mats, w1, w2]
    in_specs = [
        pl.BlockSpec((None, ts, d), lambda g: (mix_tile(g), 0, 0)),
        pl.BlockSpec((None, None, N_MOD, d),
                     lambda g: (layer, mix_tile(g) // tiles_per_seq, 0, 0)),
        pl.BlockSpec((None, None, N_MOD, d),
                     lambda g: (layer, mlp_tile(g) // tiles_per_seq, 0, 0)),
        resident(vec, j),
        *[resident(op, j) for op in mixer_mats],
        resident(w1, layer), resident(w2, layer),
    ]

    scratch = [
        pltpu.VMEM((ts, d), F32),
        pltpu.VMEM((ts, d), BF16),
    ]
    if first or final:
        scratch.append(pltpu.VMEM((d // LANES, SUBLANES * SEG_PITCH, LANES), F32))
    if first:
        scratch.append(pltpu.VMEM((ts, d), F32))
    if kind == "lru":
        width = mixer_mats[0].shape[2]
        assert width == d
        mixer_scratch = [
            pltpu.VMEM((ts, d), BF16),
            pltpu.VMEM((ts + CONV_HALO * SUBLANES, width), F32),
            pltpu.VMEM((ts, width), F32),
            pltpu.VMEM((ts, width), F32),
            pltpu.VMEM((ts, width), F32),
            pltpu.VMEM((ts, width), F32),
            pltpu.VMEM((SUBLANES, width), F32),
            pltpu.VMEM((SUBLANES, width), F32),
        ]
    else:
        mixer_scratch = [pltpu.VMEM((ts + POOL_HALO * SUBLANES, d), F32)]

    return pl.pallas_call(
        functools.partial(_layer_kernel, kind=kind, first=first, final=final,
                          tiles_per_seq=tiles_per_seq, n_mixer_weights=len(mixer_mats),
                          n_mixer_scratch=len(mixer_scratch)),
        grid=(n_tiles + 1,),
        in_specs=in_specs,
        out_specs=pl.BlockSpec((None, ts, d), lambda g: (mlp_tile(g), 0, 0)),
        out_shape=jax.ShapeDtypeStruct(x.shape, F32),
        scratch_shapes=scratch + mixer_scratch,
        compiler_params=pltpu.CompilerParams(
            dimension_semantics=("arbitrary",), vmem_limit_bytes=VMEM_LIMIT,
            allow_input_fusion=[False] * (len(operands) - 2) + [True, True]),
        name=kind + "_layer" + ("_first" if first else "") + ("_final" if final else ""),
    )(*operands)


def kernel(x, c, w_mod, b_mod, norm_mix_g, norm_ffn_g, lru_w_y, lru_b_y, lru_w_in, lru_b_in,
           lru_conv_w, lru_conv_b, lru_w_a, lru_b_a, lru_w_x, lru_b_x, lru_lambda, lru_w_out,
           lru_b_out, pool_w, pool_scale, ffn_w1, ffn_w2, final_norm_g):
    depth = w_mod.shape[0]
    bsz, seq, d = x.shape
    assert seq % SEQ_TILE == 0 and ffn_w1.shape[2] % FF_CHUNK == 0
    assert w_mod.shape[2] % MOD_COLS == 0 and d % (4 * LANES) == 0
    mod = _modulation(c, w_mod, b_mod)

    n_a, heads, hd, _ = lru_w_a.shape
    n_b = pool_w.shape[0]
    w_ax = (0.5 * jnp.concatenate([lru_w_a, lru_w_x], axis=-1)).astype(BF16)
    b_ax = (0.5 * jnp.concatenate([lru_b_a, lru_b_x], axis=-1)).reshape(n_a, -1, d)
    lru_mats = (lru_w_y.astype(BF16), lru_w_in.astype(BF16), w_ax, lru_w_out.astype(BF16))
    pool_mats = (pool_w.astype(BF16),)
    w1, w2 = ffn_w1.astype(BF16), ffn_w2.astype(BF16)

    def rows(v):
        return v[:, None, :]

    def last_norm(n):
        return jnp.broadcast_to(final_norm_g, (n, 1, d))

    assert b_ax.shape[1] == 2
    vec_lru = jnp.concatenate(
        [rows(norm_mix_g[0::N_MIXERS]), rows(norm_ffn_g[0::N_MIXERS]), rows(lru_b_y),
         rows(lru_b_in), lru_conv_w, rows(lru_conv_b), rows(lru_lambda), rows(lru_b_out), b_ax,
         last_norm(n_a), jnp.zeros((n_a, 2, d), F32)], axis=1)
    vec_pool = jnp.concatenate(
        [rows(norm_mix_g[1::N_MIXERS]), rows(norm_ffn_g[1::N_MIXERS]), rows(pool_scale),
         last_norm(n_b)], axis=1)

    x = x.reshape(bsz * seq // SEQ_TILE, SEQ_TILE, d)
    for i in range(depth):
        j = i // N_MIXERS
        if i % N_MIXERS == 0:
            kind, vec, mats = "lru", vec_lru, lru_mats
        else:
            kind, vec, mats = "pool", vec_pool, pool_mats
        x = _layer(x, mod, i, kind, vec, mats, j, w1, w2, i == 0, i == depth - 1)
    return x.reshape(bsz, seq, d)
```
